```python
import jax
import jax.numpy as jnp
from jax import lax
import numpy as np

D_MODEL = 2048
BATCH = 1
SEQ = 16384
DEPTH = 1
DEC_BATCH = 32
DEC_SEQ = 8
PAST_LEN = 16384
PAGE_SIZE = 128

NSA_HEADS = 16
NSA_KV_GROUPS = 4
HEAD_DIM = 64
HEADS_PER_GROUP = NSA_HEADS // NSA_KV_GROUPS
NSA_WIDTH = NSA_HEADS * HEAD_DIM
KV_WIDTH = NSA_KV_GROUPS * HEAD_DIM
N_NSA_BRANCHES = 3
CMP_BLOCK = 32
CMP_STRIDE = 16
SEL_BLOCK = 64
N_SELECT = 16
N_LOCAL_FORCED = 2
WINDOW = 512
Q_BLOCK = 128
POOL_WINDOWS = (2, 4, 8, 16)
POOL_GROUPS = 4
POOL_WIDTH = D_MODEL // 2
POOL_GROUP_IN = POOL_WIDTH // POOL_GROUPS
POOL_GROUP_OUT = D_MODEL // POOL_GROUPS
POOL_BUF = 15
N_MIXERS = 2
IN_SPLITS = (NSA_WIDTH, 2 * KV_WIDTH, 2 * KV_WIDTH, 2 * KV_WIDTH, NSA_HEADS * N_NSA_BRANCHES, POOL_WIDTH, N_MIXERS * D_MODEL)
IN_WIDTH = NSA_WIDTH + 6 * KV_WIDTH + NSA_HEADS * N_NSA_BRANCHES + POOL_WIDTH + N_MIXERS * D_MODEL
N_EXPERTS = 32
TOP_K = 4
D_FF = D_MODEL
SWIGLU_LIMIT = 7.0
SWIGLU_ALPHA = 1.702
MOE_BLOCK = 128
LN_EPS = 1e-5
DEEPNORM_ALPHA = (2 * DEPTH) ** 0.25
DEEPNORM_BETA = (8 * DEPTH) ** -0.25
POS_BIG = 1e30
NEG_BIG = -1e30

kernel_name = 'nsa_pool_moe_hybrid_step'


def layer_norm(x, g, b):
    xf = x.astype(jnp.float32)
    mu = jnp.mean(xf, axis=-1, keepdims=True)
    var = jnp.mean(jnp.square(xf - mu), axis=-1, keepdims=True)
    y = (xf - mu) * lax.rsqrt(var + LN_EPS) * g.astype(jnp.float32) + b.astype(jnp.float32)
    return y.astype(x.dtype)


def masked_softmax(s, mask):
    s = jnp.where(mask, s, -jnp.inf)
    m = jnp.max(s, axis=-1, keepdims=True)
    m = jnp.where(jnp.isfinite(m), m, 0.0)
    e = jnp.exp(s - m)
    return e / jnp.maximum(jnp.sum(e, axis=-1, keepdims=True), 1e-30)


def in_projection(x, w_in):
    B, T, _ = x.shape
    z = x @ w_in
    points = np.cumsum(np.array(IN_SPLITS))[:-1].tolist()
    q, kvc, kvs, kvw, ng, u, bg = jnp.split(z, points, axis=-1)
    kv_shape = (B, T, 2, NSA_KV_GROUPS, HEAD_DIM)
    return (q.reshape(B, T, NSA_HEADS, HEAD_DIM), kvc.reshape(kv_shape), kvs.reshape(kv_shape),
            kvw.reshape(kv_shape), jax.nn.sigmoid(ng.reshape(B, T, NSA_HEADS, N_NSA_BRANCHES)), u,
            jax.nn.sigmoid(bg.reshape(B, T, N_MIXERS, D_MODEL)))


def compress_kv(kv, cmp_pos, cmp_w):
    B, L = kv.shape[:2]
    n_chunk = -(-L // CMP_STRIDE)
    kv = jnp.pad(kv, ((0, 0), (0, n_chunk * CMP_STRIDE - L), (0, 0), (0, 0), (0, 0)))
    chunks = kv.reshape(B, n_chunk, CMP_STRIDE, 2, NSA_KV_GROUPS, HEAD_DIM)
    ratio = CMP_BLOCK // CMP_STRIDE
    n_blk = n_chunk - ratio + 1
    pos = cmp_pos.reshape(2, ratio, CMP_STRIDE, HEAD_DIM)
    w = cmp_w.reshape(2, ratio, CMP_STRIDE, HEAD_DIM, HEAD_DIM)
    parts = [jnp.einsum('bnscgd,csde->bncge',
                        chunks[:, r:r + n_blk] + jnp.transpose(pos[:, r], (1, 0, 2))[:, :, None, :],
                        w[:, r]) for r in range(ratio)]
    kvc = sum(parts[1:], parts[0])
    c_end = jnp.arange(n_blk, dtype=jnp.int32) * CMP_STRIDE + (CMP_BLOCK - 1)
    return kvc[:, :, 0], kvc[:, :, 1], c_end


def selection_blocks(kv):
    B, L = kv.shape[:2]
    n_sel = -(-L // SEL_BLOCK)
    kv = jnp.pad(kv, ((0, 0), (0, n_sel * SEL_BLOCK - L), (0, 0), (0, 0), (0, 0)))
    return kv.reshape(B, n_sel, SEL_BLOCK, 2, NSA_KV_GROUPS, HEAD_DIM)


def overlap_matrix(n_cmp, n_sel):
    start_c = jnp.arange(n_cmp)[:, None] * CMP_STRIDE
    start_s = jnp.arange(n_sel)[None, :] * SEL_BLOCK
    return ((start_c < start_s + SEL_BLOCK) & (start_c + CMP_BLOCK > start_s)).astype(jnp.float32)


def nsa_attend(q, q_pos, gates, k_cmp, v_cmp, c_end, sel_blocks, kv_win, w_pos):
    B, Q = q.shape[:2]
    G, R = NSA_KV_GROUPS, HEADS_PER_GROUP
    scale = HEAD_DIM ** -0.5
    qg = q.reshape(B, Q, G, R, HEAD_DIM)
    s = jnp.einsum('bqgrd,bngd->bgrqn', qg, k_cmp).astype(jnp.float32) * scale
    p_cmp = masked_softmax(s, (c_end[None, :] <= q_pos[:, None])[None, None, None])
    o_cmp = jnp.einsum('bgrqn,bngd->bqgrd', p_cmp.astype(v_cmp.dtype), v_cmp)
    n_sel = sel_blocks.shape[1]
    imp = jnp.einsum('bgrqn,ns->bgqs', p_cmp, overlap_matrix(k_cmp.shape[1], n_sel))
    blk = jnp.arange(n_sel)
    dist = (q_pos // SEL_BLOCK)[:, None] - blk[None, :]
    forced = (blk[None, :] == 0) | ((dist >= 0) & (dist < N_LOCAL_FORCED))
    causal = blk[None, :] * SEL_BLOCK <= q_pos[:, None]
    imp = jnp.where(causal, jnp.where(forced, POS_BIG, imp), NEG_BIG)
    _, idx = lax.top_k(imp, min(N_SELECT, n_sel))
    n_pick = idx.shape[-1]
    bi = jnp.arange(B)[:, None, None, None]
    gi = jnp.arange(G)[None, :, None, None]
    sel = sel_blocks[bi, idx, :, :, gi, :]
    key_pos = idx[..., None] * SEL_BLOCK + jnp.arange(SEL_BLOCK)
    m_sel = key_pos <= q_pos[None, None, :, None, None]
    s = jnp.einsum('bqgrd,bgqnsd->bgrqns', qg, sel[..., 0, :]).astype(jnp.float32) * scale
    p = masked_softmax(s.reshape(B, G, R, Q, n_pick * SEL_BLOCK),
                       m_sel.reshape(B, G, 1, Q, n_pick * SEL_BLOCK))
    o_sel = jnp.einsum('bgrqns,bgqnsd->bqgrd', p.reshape(s.shape).astype(sel.dtype), sel[..., 1, :])
    s = jnp.einsum('bqgrd,blgd->bgrql', qg, kv_win[:, :, 0]).astype(jnp.float32) * scale
    m_win = ((w_pos[None, :] <= q_pos[:, None]) & (w_pos[None, :] > q_pos[:, None] - WINDOW)
             & (w_pos[None, :] >= 0))
    p = masked_softmax(s, m_win[None, None, None])
    o_win = jnp.einsum('bgrql,blgd->bqgrd', p.astype(kv_win.dtype), kv_win[:, :, 1])
    g = gates.reshape(B, Q, G, R, N_NSA_BRANCHES)
    o = g[..., 0:1] * o_cmp + g[..., 1:2] * o_sel + g[..., 2:3] * o_win
    return o.reshape(B, Q, NSA_WIDTH)


def nsa_prompt(q, gates, kv_cmp, kv_sel, kv_win, cmp_pos, cmp_w):
    B, T = q.shape[:2]
    k_cmp, v_cmp, c_end = compress_kv(kv_cmp, cmp_pos, cmp_w)
    sel_blocks = selection_blocks(kv_sel)
    kv_win_pad = jnp.pad(kv_win, ((0, 0), (WINDOW, 0), (0, 0), (0, 0), (0, 0)))
    n_qb = T // Q_BLOCK
    q_blocks = jnp.swapaxes(q.reshape(B, n_qb, Q_BLOCK, NSA_HEADS, HEAD_DIM), 0, 1)
    g_blocks = jnp.swapaxes(gates.reshape(B, n_qb, Q_BLOCK, NSA_HEADS, N_NSA_BRANCHES), 0, 1)
    starts = jnp.arange(n_qb, dtype=jnp.int32) * Q_BLOCK

    def one_block(args):
        start, q_blk, g_blk = args
        q_pos = start + jnp.arange(Q_BLOCK, dtype=jnp.int32)
        ctx = lax.dynamic_slice_in_dim(kv_win_pad, start, WINDOW + Q_BLOCK, axis=1)
        w_pos = start - WINDOW + jnp.arange(WINDOW + Q_BLOCK, dtype=jnp.int32)
        return nsa_attend(q_blk, q_pos, g_blk, k_cmp, v_cmp, c_end, sel_blocks, ctx, w_pos)

    o = lax.map(one_block, (starts, q_blocks, g_blocks))
    return jnp.swapaxes(o, 0, 1).reshape(B, T, NSA_WIDTH)


def nsa_sample(q, gates, kv_cmp_new, kv_sel_new, kv_win_new, past_cmp, past_sel, win_buf, cmp_pos, cmp_w):
    P = past_cmp.shape[1]
    T = q.shape[1]
    Wb = win_buf.shape[1]
    k_cmp, v_cmp, c_end = compress_kv(jnp.concatenate([past_cmp, kv_cmp_new], axis=1), cmp_pos, cmp_w)
    sel_blocks = selection_blocks(jnp.concatenate([past_sel, kv_sel_new], axis=1))
    ctx = jnp.concatenate([win_buf, kv_win_new], axis=1)
    q_pos = P + jnp.arange(T, dtype=jnp.int32)
    w_pos = P - Wb + jnp.arange(Wb + T, dtype=jnp.int32)
    o = nsa_attend(q, q_pos, gates, k_cmp, v_cmp, c_end, sel_blocks, ctx, w_pos)
    return o, ctx[:, T:]


def pool_mixer(u, prev, pool_w, pool_b, pool_scale):
    B, T, _ = u.shape
    P = prev.shape[1]
    ctx = jnp.concatenate([prev, u], axis=1)
    csum = jnp.pad(jnp.cumsum(ctx.astype(jnp.float32), axis=1), ((0, 0), (1, 0), (0, 0)))
    end_idx = P + 1 + jnp.arange(T)
    cur = ctx[:, P:].astype(jnp.float32)
    outs = []
    for g, w in enumerate(POOL_WINDOWS):
        lo, hi = g * POOL_GROUP_IN, (g + 1) * POOL_GROUP_IN
        start_idx = jnp.maximum(end_idx - w, 0)
        cnt = (end_idx - start_idx).astype(jnp.float32)[None, :, None]
        mean = (csum[:, P + 1:, lo:hi] - csum[:, start_idx, lo:hi]) / cnt
        pooled = (mean - cur[..., lo:hi]).astype(u.dtype)
        outs.append(pooled @ pool_w[g] + pool_b[g])
    return jnp.concatenate(outs, axis=-1) * pool_scale, ctx[:, -POOL_BUF:]


def merge_branches(o_nsa, b_pool, bg, w_branch_a, w_out):
    a = o_nsa @ w_branch_a
    return (bg[..., 0, :] * a + bg[..., 1, :] * b_pool) @ w_out


def moe_ffn(h, router_w, router_b, w_up, b_up, w_down, b_down):
    shape = h.shape
    xt = h.reshape(-1, D_MODEL)
    T = xt.shape[0]
    logits = (xt @ router_w).astype(jnp.float32) + router_b.astype(jnp.float32)
    top_l, top_e = lax.top_k(logits, TOP_K)
    gate = jax.nn.softmax(top_l, axis=-1)
    e_flat = top_e.reshape(-1)
    g_flat = gate.reshape(-1)
    tok_flat = jnp.repeat(jnp.arange(T, dtype=jnp.int32), TOP_K)
    order = jnp.argsort(e_flat)
    e_sorted = e_flat[order]
    counts = jnp.bincount(e_flat, length=N_EXPERTS)
    padded = (counts + MOE_BLOCK - 1) // MOE_BLOCK * MOE_BLOCK
    pad_end = jnp.cumsum(padded)
    pad_start = pad_end - padded
    start = jnp.cumsum(counts) - counts
    dest = pad_start[e_sorted] + (jnp.arange(T * TOP_K) - start[e_sorted])
    n_rows = (-(-(T * TOP_K) // MOE_BLOCK) + N_EXPERTS) * MOE_BLOCK
    row_tok = jnp.full((n_rows,), T, dtype=jnp.int32).at[dest].set(tok_flat[order])
    row_gate = jnp.zeros((n_rows,), jnp.float32).at[dest].set(g_flat[order])
    n_blocks = n_rows // MOE_BLOCK
    blk_expert = jnp.clip(jnp.searchsorted(pad_end, jnp.arange(n_blocks) * MOE_BLOCK, side='right'),
                          0, N_EXPERTS - 1)
    x_pad = jnp.concatenate([xt, jnp.zeros((1, D_MODEL), xt.dtype)], axis=0)

    def expert_block(args):
        rows, e = args
        hu = x_pad[rows] @ w_up[e] + b_up[e]
        gt = jnp.minimum(hu[:, :D_FF], SWIGLU_LIMIT)
        up = jnp.clip(hu[:, D_FF:], -SWIGLU_LIMIT, SWIGLU_LIMIT)
        act = (up + 1.0) * gt * jax.nn.sigmoid(SWIGLU_ALPHA * gt)
        return act @ w_down[e] + b_down[e]

    out_rows = lax.map(expert_block, (row_tok.reshape(n_blocks, MOE_BLOCK), blk_expert))
    out_rows = out_rows.reshape(n_rows, D_MODEL) * row_gate[:, None].astype(xt.dtype)
    y = jnp.zeros((T + 1, D_MODEL), xt.dtype).at[row_tok].add(out_rows)[:T]
    return y.reshape(shape)


def post_norm_block(x, mix, ln1_g, ln1_b, router_w, router_b, w_up, b_up, w_down, b_down, ln2_g, ln2_b):
    h = layer_norm(DEEPNORM_ALPHA * x + mix, ln1_g, ln1_b)
    return layer_norm(DEEPNORM_ALPHA * h + moe_ffn(h, router_w, router_b, w_up, b_up, w_down, b_down), ln2_g, ln2_b)


def decoder_layer(x_p, x_s, past_cmp, past_sel, win_buf, pool_buf, w_in, cmp_pos, cmp_w, w_branch_a,
                  pool_w, pool_b, pool_scale, w_out, ln1_g, ln1_b, router_w, router_b, w_up, b_up,
                  w_down, b_down, ln2_g, ln2_b):
    q, kvc_p, kvs_p, kvw_p, ng, u, bg = in_projection(x_p, w_in)
    o = nsa_prompt(q, ng, kvc_p, kvs_p, kvw_p, cmp_pos, cmp_w)
    pb, pool_p = pool_mixer(u, u[:, :0], pool_w, pool_b, pool_scale)
    mix = merge_branches(o, pb, bg, w_branch_a, w_out)
    y_p = post_norm_block(x_p, mix, ln1_g, ln1_b, router_w, router_b, w_up, b_up, w_down, b_down, ln2_g, ln2_b)
    win_p = kvw_p[:, -min(WINDOW, x_p.shape[1]):]
    q, kvc_s, kvs_s, kvw_s, ng, u, bg = in_projection(x_s, w_in)
    o, win_s = nsa_sample(q, ng, kvc_s, kvs_s, kvw_s, past_cmp, past_sel, win_buf, cmp_pos, cmp_w)
    pb, pool_s = pool_mixer(u, pool_buf, pool_w, pool_b, pool_scale)
    mix = merge_branches(o, pb, bg, w_branch_a, w_out)
    y_s = post_norm_block(x_s, mix, ln1_g, ln1_b, router_w, router_b, w_up, b_up, w_down, b_down, ln2_g, ln2_b)
    return (y_p, y_s, kvc_p, kvc_s, kvs_p, kvs_s, win_p, win_s, pool_p, pool_s)


def setup_inputs(seed: int = 0) -> dict:
    key = jax.random.key(seed)
    ks = jax.random.split(key, 24)
    n_pages = PAST_LEN // PAGE_SIZE
    n_used = DEC_BATCH * n_pages
    n_phys = n_used + n_used // 4
    win_buf = min(WINDOW, PAST_LEN)

    def nrm(k, shape, scale):
        return jax.random.normal(k, shape, jnp.float32) * scale

    page_table = jax.random.permutation(ks[0], n_phys)[:n_used].reshape(DEC_BATCH, n_pages).astype(jnp.int32)
    kv_page_shape = (DEPTH, n_phys, PAGE_SIZE, 2, NSA_KV_GROUPS, HEAD_DIM)
    return {
        'x_prompt': nrm(ks[1], (BATCH, SEQ, D_MODEL), 1.0),
        'x_sample': nrm(ks[2], (DEC_BATCH, DEC_SEQ, D_MODEL), 1.0),
        'cache_cmp_kv': nrm(ks[3], kv_page_shape, 1.0),
        'cache_sel_kv': nrm(ks[4], kv_page_shape, 1.0),
        'state_win_kv': nrm(ks[5], (DEPTH, DEC_BATCH, win_buf, 2, NSA_KV_GROUPS, HEAD_DIM), 1.0),
        'state_pool': nrm(ks[6], (DEPTH, DEC_BATCH, POOL_BUF, POOL_WIDTH), 1.0),
        'page_table': page_table,
        'w_in': nrm(ks[7], (DEPTH, D_MODEL, IN_WIDTH), D_MODEL ** -0.5),
        'cmp_pos': nrm(ks[8], (DEPTH, 2, CMP_BLOCK, HEAD_DIM), 0.1),
        'cmp_w': nrm(ks[9], (DEPTH, 2, CMP_BLOCK, HEAD_DIM, HEAD_DIM), (CMP_BLOCK * HEAD_DIM) ** -0.5),
        'w_branch_a': nrm(ks[10], (DEPTH, NSA_WIDTH, D_MODEL), NSA_WIDTH ** -0.5),
        'pool_w': nrm(ks[11], (DEPTH, POOL_GROUPS, POOL_GROUP_IN, POOL_GROUP_OUT), POOL_GROUP_IN ** -0.5),
        'pool_b': nrm(ks[12], (DEPTH, POOL_GROUPS, POOL_GROUP_OUT), 0.01),
        'pool_scale': 1.0 + nrm(ks[13], (DEPTH, D_MODEL), 0.1),
        'w_out': nrm(ks[14], (DEPTH, D_MODEL, D_MODEL), D_MODEL ** -0.5 * DEEPNORM_BETA),
        'ln1_g': 1.0 + nrm(ks[15], (DEPTH, D_MODEL), 0.1),
        'ln1_b': nrm(ks[16], (DEPTH, D_MODEL), 0.01),
        'router_w': nrm(ks[17], (DEPTH, D_MODEL, N_EXPERTS), D_MODEL ** -0.5),
        'router_b': nrm(ks[18], (DEPTH, N_EXPERTS), 0.01),
        'w_up': nrm(ks[19], (DEPTH, N_EXPERTS, D_MODEL, 2 * D_FF), D_MODEL ** -0.5),
        'b_up': nrm(ks[20], (DEPTH, N_EXPERTS, 2 * D_FF), 0.01),
        'w_down': nrm(ks[21], (DEPTH, N_EXPERTS, D_FF, D_MODEL), D_FF ** -0.5 * DEEPNORM_BETA),
        'b_down': nrm(ks[22], (DEPTH, N_EXPERTS, D_MODEL), 0.01),
        'ln2_g': 1.0 + nrm(ks[23], (DEPTH, D_MODEL), 0.1),
        'ln2_b': nrm(jax.random.fold_in(key, 99), (DEPTH, D_MODEL), 0.01),
    }


def reference(x_prompt, x_sample, cache_cmp_kv, cache_sel_kv, state_win_kv, state_pool, page_table,
              w_in, cmp_pos, cmp_w, w_branch_a, pool_w, pool_b, pool_scale, w_out, ln1_g, ln1_b,
              router_w, router_b, w_up, b_up, w_down, b_down, ln2_g, ln2_b):
    n_seq, n_pages = page_table.shape
    xp, xs = x_prompt, x_sample
    outs = []
    for l in range(DEPTH):
        past_cmp = cache_cmp_kv[l, page_table].reshape(n_seq, n_pages * PAGE_SIZE, 2, NSA_KV_GROUPS, HEAD_DIM)
        past_sel = cache_sel_kv[l, page_table].reshape(n_seq, n_pages * PAGE_SIZE, 2, NSA_KV_GROUPS, HEAD_DIM)
        res = decoder_layer(xp, xs, past_cmp, past_sel, state_win_kv[l], state_pool[l], w_in[l], cmp_pos[l],
                            cmp_w[l], w_branch_a[l], pool_w[l], pool_b[l], pool_scale[l], w_out[l], ln1_g[l],
                            ln1_b[l], router_w[l], router_b[l], w_up[l], b_up[l], w_down[l], b_down[l],
                            ln2_g[l], ln2_b[l])
        xp, xs = res[0], res[1]
        outs.append(res[2:])
    new_cmp_p = jnp.stack([o[0] for o in outs])
    new_cmp_s = jnp.stack([o[1] for o in outs])
    new_sel_p = jnp.stack([o[2] for o in outs])
    new_sel_s = jnp.stack([o[3] for o in outs])
    win_p = jnp.stack([o[4] for o in outs])
    win_s = jnp.stack([o[5] for o in outs])
    pool_p = jnp.stack([o[6] for o in outs])
    pool_s = jnp.stack([o[7] for o in outs])
    return (xp, xs, new_cmp_p, new_cmp_s, new_sel_p, new_sel_s, win_p, win_s, pool_p, pool_s)
```

```python
import functools

import jax
import jax.numpy as jnp
import numpy as np
from jax import lax
from jax.experimental import pallas as pl
from jax.experimental.pallas import tpu as pltpu

F32 = jnp.float32
BF16 = jnp.bfloat16
I32 = jnp.int32

D_MODEL = 2048
N_HEADS = 16
N_GROUPS = 4
HEAD_DIM = 64
HEADS_PER_GROUP = N_HEADS // N_GROUPS
NSA_WIDTH = N_HEADS * HEAD_DIM
KV_WIDTH = N_GROUPS * HEAD_DIM
KV_ROW = 2 * KV_WIDTH
N_BRANCH = 3
CMP_BLOCK = 32
CMP_STRIDE = 16
SEL_BLOCK = 64
N_SELECT = 16
N_LOCAL_FORCED = 2
WINDOW = 512
POOL_WINDOWS = (2, 4, 8, 16)
POOL_GROUPS = 4
POOL_WIDTH = D_MODEL // 2
POOL_GROUP_IN = POOL_WIDTH // POOL_GROUPS
POOL_GROUP_OUT = D_MODEL // POOL_GROUPS
POOL_BUF = 15
N_MIXERS = 2
TOP_K = 4
SWIGLU_LIMIT = 7.0
SWIGLU_ALPHA = 1.702
LN_EPS = 1e-5
POS_BIG = 1e30
NEG_BIG = -1e30

LANES = 128
SUBLANES = 8
VMEM_LIMIT = 56 * 1024 * 1024

PAGE = 128
QB = 128
CHUNKS_PER_PAGE = PAGE // CMP_STRIDE


def _cparams(sem, vmem=VMEM_LIMIT):
    return pltpu.CompilerParams(dimension_semantics=sem, vmem_limit_bytes=vmem)


SEG = 512
_SEG_Q, _SEG_KVC, _SEG_KVS, _SEG_KVW, _SEG_NG, _SEG_U, _SEG_BG = 0, 2, 3, 4, 5, 6, 8
N_SEG = 16


def _inproj_body(x_ref, w_ref, qT_ref, kvc_ref, kvs_ref, ks_ref, vsT_ref, kvw_ref, kw_ref, vwT_ref,
                 gT_ref, u_ref, bg_ref, xb_ref):
    j = pl.program_id(1)

    @pl.when(j == 0)
    def _():
        xb_ref[...] = x_ref[...].astype(BF16)

    z = jnp.dot(xb_ref[...], w_ref[...], preferred_element_type=F32)

    @pl.when(j < _SEG_KVC)
    def _():
        qT_ref[...] = (z * (HEAD_DIM ** -0.5)).T.astype(BF16)

    @pl.when(j == _SEG_KVC)
    def _():
        kvc_ref[...] = z

    @pl.when(j == _SEG_KVS)
    def _():
        kvs_ref[...] = z
        ks_ref[...] = z[:, :KV_WIDTH].astype(BF16)
        vsT_ref[...] = z[:, KV_WIDTH:].T.astype(BF16)

    @pl.when(j == _SEG_KVW)
    def _():
        kvw_ref[...] = z
        kw_ref[...] = z[:, :KV_WIDTH].astype(BF16)
        vwT_ref[...] = z[:, KV_WIDTH:].T.astype(BF16)

    @pl.when(j == _SEG_NG)
    def _():
        gT_ref[...] = jax.nn.sigmoid(z[:, :LANES]).T

    @pl.when((j >= _SEG_U) & (j < _SEG_BG))
    def _():
        u_ref[...] = z

    @pl.when(j >= _SEG_BG)
    def _():
        bg_ref[...] = jax.nn.sigmoid(z)


def _prep_w_in(w_in):
    sizes = (NSA_WIDTH, KV_ROW, KV_ROW, KV_ROW, N_HEADS * N_BRANCH, POOL_WIDTH, N_MIXERS * D_MODEL)
    offs = np.concatenate([[0], np.cumsum(sizes)])
    parts = []
    for k, s in enumerate(sizes):
        p = w_in[:, offs[k]:offs[k + 1]]
        pad = (-s) % SEG
        if pad:
            p = jnp.pad(p, ((0, 0), (0, pad)))
        parts.append(p)
    return jnp.concatenate(parts, axis=1).astype(BF16)


def in_projection(x, w_cat, tm):
    T, D = x.shape
    assert T % tm == 0 and tm % LANES == 0
    grid = (T // tm, N_SEG)
    row = lambda i, j: (i, 0)
    colT = lambda i, j: (0, i)
    out_shape = (
        jax.ShapeDtypeStruct((NSA_WIDTH, T), BF16),
        jax.ShapeDtypeStruct((T, KV_ROW), F32),
        jax.ShapeDtypeStruct((T, KV_ROW), F32),
        jax.ShapeDtypeStruct((T, KV_WIDTH), BF16),
        jax.ShapeDtypeStruct((KV_WIDTH, T), BF16),
        jax.ShapeDtypeStruct((T, KV_ROW), F32),
        jax.ShapeDtypeStruct((T, KV_WIDTH), BF16),
        jax.ShapeDtypeStruct((KV_WIDTH, T), BF16),
        jax.ShapeDtypeStruct((LANES, T), F32),
        jax.ShapeDtypeStruct((T, POOL_WIDTH), F32),
        jax.ShapeDtypeStruct((T, N_MIXERS * D_MODEL), F32),
    )
    out_specs = (
        pl.BlockSpec((SEG, tm), lambda i, j: (jnp.minimum(j, _SEG_KVC - 1), i)),
        pl.BlockSpec((tm, KV_ROW), row),
        pl.BlockSpec((tm, KV_ROW), row),
        pl.BlockSpec((tm, KV_WIDTH), row),
        pl.BlockSpec((KV_WIDTH, tm), colT),
        pl.BlockSpec((tm, KV_ROW), row),
        pl.BlockSpec((tm, KV_WIDTH), row),
        pl.BlockSpec((KV_WIDTH, tm), colT),
        pl.BlockSpec((LANES, tm), colT),
        pl.BlockSpec((tm, SEG), lambda i, j: (i, jnp.clip(j - _SEG_U, 0, _SEG_BG - _SEG_U - 1))),
        pl.BlockSpec((tm, SEG), lambda i, j: (i, jnp.clip(j - _SEG_BG, 0, N_SEG - _SEG_BG - 1))),
    )
    return pl.pallas_call(
        _inproj_body,
        out_shape=out_shape,
        grid=grid,
        in_specs=[pl.BlockSpec((tm, D), row), pl.BlockSpec((D, SEG), lambda i, j: (0, j))],
        out_specs=out_specs,
        scratch_shapes=[pltpu.VMEM((tm, D), BF16)],
        compiler_params=_cparams(("parallel", "arbitrary")),
        name="in_projection",
    )(x, w_cat)


PAGES_PER_STEP = 16
_LT = KV_ROW // LANES


def _compress_body(pt_ref, *refs):
    del pt_ref
    pages = refs[:PAGES_PER_STEP]
    pos_ref, bdk_ref, bdv_ref, out_ref, vT_ref, sh_ref = refs[PAGES_PER_STEP:]
    st = pl.program_id(1)
    n = PAGES_PER_STEP * CHUNKS_PER_PAGE

    @pl.when(st == 0)
    def _():
        sh_ref[0:SUBLANES, :] = jnp.zeros((SUBLANES, KV_ROW), F32)

    acc = [jnp.zeros((n, KV_ROW), F32), jnp.zeros((n, KV_ROW), F32)]
    for j in range(CMP_STRIDE):
        xj = jnp.concatenate(
            [jnp.concatenate([p[0, pl.ds(_LT * j + c, CHUNKS_PER_PAGE, stride=_LT * CMP_STRIDE), :] for p in pages], axis=0)
             for c in range(_LT)], axis=1)
        for r in range(CMP_BLOCK // CMP_STRIDE):
            s = r * CMP_STRIDE + j
            lhs = (xj + pos_ref[s:s + 1, :]).astype(BF16)
            kk = jnp.dot(lhs[:, :KV_WIDTH], bdk_ref[s], preferred_element_type=F32)
            vv = jnp.dot(lhs[:, KV_WIDTH:], bdv_ref[s], preferred_element_type=F32)
            acc[r] = acc[r] + jnp.concatenate([kk, vv], axis=1)
    sh_ref[SUBLANES:SUBLANES + n, :] = acc[0]
    res = sh_ref[SUBLANES - 1:SUBLANES - 1 + n, :] + acc[1]
    sh_ref[SUBLANES - 1:SUBLANES, :] = acc[0][n - 1:n, :]
    out_ref[0] = res.astype(BF16)
    vT_ref[0] = res[:, KV_WIDTH:].T.astype(BF16)


def _prep_cmp(cmp_pos, cmp_w):
    pos_rows = jnp.concatenate([jnp.tile(cmp_pos[0], (1, N_GROUPS)), jnp.tile(cmp_pos[1], (1, N_GROUPS))], axis=1)
    eye = jnp.eye(N_GROUPS, dtype=cmp_w.dtype)
    bd = jnp.einsum('gh,csde->csgdhe', eye, cmp_w).reshape(2, CMP_BLOCK, KV_WIDTH, KV_WIDTH).astype(BF16)
    return pos_rows.astype(F32), bd[0], bd[1]


def compress_kv(src_pages, table, pos_rows, bdk, bdv):
    B, n_pages = table.shape
    assert n_pages % PAGES_PER_STEP == 0
    n_steps = n_pages // PAGES_PER_STEP
    n = PAGES_PER_STEP * CHUNKS_PER_PAGE
    flat = table.reshape(-1).astype(I32)

    def page_map(k):
        return lambda b, st, pt: (pt[b * n_pages + st * PAGES_PER_STEP + k], 0, 0)

    const2 = lambda b, st, pt: (0, 0)
    const3 = lambda b, st, pt: (0, 0, 0)
    src_pages = src_pages.reshape(src_pages.shape[0], PAGE * _LT, LANES)
    in_specs = [pl.BlockSpec((1, PAGE * _LT, LANES), page_map(k)) for k in range(PAGES_PER_STEP)]
    in_specs += [pl.BlockSpec((CMP_BLOCK, KV_ROW), const2),
                 pl.BlockSpec((CMP_BLOCK, KV_WIDTH, KV_WIDTH), const3),
                 pl.BlockSpec((CMP_BLOCK, KV_WIDTH, KV_WIDTH), const3)]
    grid_spec = pltpu.PrefetchScalarGridSpec(
        num_scalar_prefetch=1,
        grid=(B, n_steps),
        in_specs=in_specs,
        out_specs=(pl.BlockSpec((1, n, KV_ROW), lambda b, st, pt: (b, st, 0)),
                   pl.BlockSpec((1, KV_WIDTH, n), lambda b, st, pt: (b, 0, st))),
        scratch_shapes=[pltpu.VMEM((n + 2 * SUBLANES, KV_ROW), F32)],
    )
    return pl.pallas_call(
        _compress_body,
        out_shape=(jax.ShapeDtypeStruct((B, n_pages * CHUNKS_PER_PAGE, KV_ROW), BF16),
                   jax.ShapeDtypeStruct((B, KV_WIDTH, n_pages * CHUNKS_PER_PAGE), BF16)),
        grid_spec=grid_spec,
        compiler_params=_cparams(("parallel", "arbitrary")),
        name="compress_kv",
    )(flat, *([src_pages] * PAGES_PER_STEP), pos_rows, bdk, bdv)


GQ = HEADS_PER_GROUP * QB
PAIR_W = 2 * HEAD_DIM


def _q_aug(qT_ref, g):
    qg = jnp.concatenate([qT_ref[(HEADS_PER_GROUP * g + r) * HEAD_DIM:(HEADS_PER_GROUP * g + r + 1) * HEAD_DIM, :]
                          for r in range(HEADS_PER_GROUP)], axis=1)
    z = jnp.zeros_like(qg)
    return jnp.concatenate([qg, z], axis=0) if g % 2 == 0 else jnp.concatenate([z, qg], axis=0)


def _qpos_lanes(qb, width):
    lane = lax.broadcasted_iota(I32, (1, width), 1)
    return qb * QB + (lane & (QB - 1))


def _store_heads(o_ref, g, accT, scale_row, gT_ref, branch):
    for r in range(HEADS_PER_GROUP):
        h = HEADS_PER_GROUP * g + r
        gate = gT_ref[h * N_BRANCH + branch:h * N_BRANCH + branch + 1, :]
        cols = slice(r * QB, (r + 1) * QB)
        o_ref[h * HEAD_DIM:(h + 1) * HEAD_DIM, :] = accT[:, cols] * (scale_row[:, cols] * gate)


def _cmp_select_body(qT_ref, gT_ref, kc_ref, vcT_ref, ovT_ref, o_ref, selT_ref, *, n_select):
    qb = pl.program_id(0)
    n_cmp = kc_ref.shape[1]
    n_sel = ovT_ref.shape[0]
    qpos = _qpos_lanes(qb, GQ)
    m_idx = lax.broadcasted_iota(I32, (n_cmp, 1), 0)
    usable = (m_idx >= 1) & (m_idx * CMP_STRIDE + (CMP_BLOCK - CMP_STRIDE - 1) <= qpos)
    blk = lax.broadcasted_iota(I32, (n_sel, QB), 0)
    qp = qpos[:, :QB]
    causal = blk * SEL_BLOCK <= qp
    dist = qp // SEL_BLOCK - blk
    forced = (blk == 0) | ((dist >= 0) & (dist < N_LOCAL_FORCED))
    for g in range(N_GROUPS):
        pair = g // 2
        sT = jnp.dot(kc_ref[0, :, pair * PAIR_W:(pair + 1) * PAIR_W], _q_aug(qT_ref, g),
                     preferred_element_type=F32)
        sT = jnp.where(usable, sT, -jnp.inf)
        mx = jnp.max(sT, axis=0, keepdims=True)
        mx = jnp.where(mx == -jnp.inf, 0.0, mx)
        e = jnp.exp(sT - mx)
        inv = 1.0 / jnp.maximum(jnp.sum(e, axis=0, keepdims=True), 1e-30)
        pb = (e * inv).astype(BF16)
        oT = jnp.dot(vcT_ref[0, g * HEAD_DIM:(g + 1) * HEAD_DIM, :], pb, preferred_element_type=F32)
        _store_heads(o_ref, g, oT, jnp.ones((1, GQ), F32), gT_ref, 0)
        imp = jnp.zeros((n_sel, QB), F32)
        for r in range(HEADS_PER_GROUP):
            imp = imp + jnp.dot(ovT_ref[...], pb[:, r * QB:(r + 1) * QB], preferred_element_type=F32)
        adj = jnp.where(causal, jnp.where(forced, POS_BIG, imp), NEG_BIG)

        def pick(_, carry):
            adj, sel = carry
            mx = jnp.max(adj, axis=0, keepdims=True)
            first = jnp.min(jnp.where(adj == mx, blk, n_sel), axis=0, keepdims=True)
            hit = blk == first
            return jnp.where(hit, -jnp.inf, adj), jnp.where(hit, 1.0, sel)

        _, sel = lax.fori_loop(0, n_select, pick, (adj, jnp.zeros((n_sel, QB), F32)))
        selT_ref[g] = sel


def _overlap_T(n_cmp_rows, n_sel):
    m = np.arange(n_cmp_rows)[None, :] - 1
    s = np.arange(n_sel)[:, None]
    start_c = m * CMP_STRIDE
    ov = (m >= 0) & (start_c < s * SEL_BLOCK + SEL_BLOCK) & (start_c + CMP_BLOCK > s * SEL_BLOCK)
    return jnp.asarray(ov, BF16)


def prompt_cmp_select(qT, gT, kc, vcT, n_q):
    n_cmp = kc.shape[1]
    n_sel = n_q // SEL_BLOCK
    ovT = _overlap_T(n_cmp, n_sel)
    col = lambda i: (0, i)
    return pl.pallas_call(
        functools.partial(_cmp_select_body, n_select=min(N_SELECT, n_sel)),
        out_shape=(jax.ShapeDtypeStruct((NSA_WIDTH, n_q), F32), jax.ShapeDtypeStruct((N_GROUPS, n_sel, n_q), F32)),
        grid=(n_q // QB,),
        in_specs=[pl.BlockSpec((NSA_WIDTH, QB), col), pl.BlockSpec((LANES, QB), col),
                  pl.BlockSpec((1, n_cmp, KV_ROW), lambda i: (0, 0, 0)),
                  pl.BlockSpec((1, KV_WIDTH, n_cmp), lambda i: (0, 0, 0)),
                  pl.BlockSpec((n_sel, n_cmp), lambda i: (0, 0))],
        out_specs=(pl.BlockSpec((NSA_WIDTH, QB), col), pl.BlockSpec((N_GROUPS, n_sel, QB), lambda i: (0, 0, i))),
        compiler_params=_cparams(("parallel",)),
        name="prompt_cmp_select",
    )(qT, gT, kc, vcT, ovT)


KT = 512
MASKED = -1e30


def _online_softmax_step(sT, valid, m, l, acc, vT_tile):
    s = jnp.where(valid, sT, MASKED)
    m_new = jnp.maximum(m, jnp.max(s, axis=0, keepdims=True))
    alpha = jnp.exp(m - m_new)
    e = jnp.where(valid, jnp.exp(s - m_new), 0.0)
    l = l * alpha + jnp.sum(e, axis=0, keepdims=True)
    acc = acc * alpha + jnp.dot(vT_tile, e.astype(BF16), preferred_element_type=F32)
    return m_new, l, acc


def _softmax_init():
    return (jnp.full((1, GQ), MASKED, F32), jnp.zeros((1, GQ), F32), jnp.zeros((HEAD_DIM, GQ), F32))


def _prompt_sel_body(qT_ref, gT_ref, selT_ref, k_ref, vT_ref, o_ref):
    qb = pl.program_id(0)
    n_kt = (qb * QB + QB - 1) // KT + 1
    qpos = _qpos_lanes(qb, GQ)
    krow = lax.broadcasted_iota(I32, (KT, 1), 0)
    blocks_per_tile = KT // SEL_BLOCK
    for g in range(N_GROUPS):
        pair = g // 2
        qa = _q_aug(qT_ref, g)

        def tile(kt, carry, g=g, pair=pair, qa=qa):
            base = pl.multiple_of(kt * KT, KT)
            sT = jnp.dot(k_ref[pl.ds(base, KT), pair * PAIR_W:(pair + 1) * PAIR_W], qa, preferred_element_type=F32)
            rows = selT_ref[g, pl.ds(pl.multiple_of(kt * blocks_per_tile, blocks_per_tile), blocks_per_tile), :]
            picked = jnp.broadcast_to(rows[:, None, :], (blocks_per_tile, SEL_BLOCK, QB)).reshape(KT, QB)
            picked = jnp.concatenate([picked] * HEADS_PER_GROUP, axis=1)
            valid = (picked > 0.5) & (base + krow <= qpos)
            vt = vT_ref[g * HEAD_DIM:(g + 1) * HEAD_DIM, pl.ds(base, KT)]
            return _online_softmax_step(sT, valid, *carry, vt)

        m, l, acc = lax.fori_loop(0, n_kt, tile, _softmax_init())
        _store_heads(o_ref, g, acc, 1.0 / jnp.maximum(l, 1e-30), gT_ref, 1)


def prompt_sel_attention(qT, gT, selT, ks, vsT, n_q):
    n_sel = selT.shape[1]
    col = lambda i: (0, i)
    return pl.pallas_call(
        _prompt_sel_body,
        out_shape=jax.ShapeDtypeStruct((NSA_WIDTH, n_q), F32),
        grid=(n_q // QB,),
        in_specs=[pl.BlockSpec((NSA_WIDTH, QB), col), pl.BlockSpec((LANES, QB), col),
                  pl.BlockSpec((N_GROUPS, n_sel, QB), lambda i: (0, 0, i)),
                  pl.BlockSpec((n_q, KV_WIDTH), lambda i: (0, 0)),
                  pl.BlockSpec((KV_WIDTH, n_q), lambda i: (0, 0))],
        out_specs=pl.BlockSpec((NSA_WIDTH, QB), col),
        compiler_params=_cparams(("parallel",)),
        name="prompt_sel_attention",
    )(qT, gT, selT, ks, vsT)


WT = 128


def _prompt_win_body(qT_ref, gT_ref, k_ref, vT_ref, o_ref):
    qb = pl.program_id(0)
    qpos = _qpos_lanes(qb, GQ)
    krow = lax.broadcasted_iota(I32, (WT, 1), 0)
    n_tiles = WINDOW // WT + QB // WT
    first = jnp.maximum(qb * (QB // WT) - WINDOW // WT, 0)
    for g in range(N_GROUPS):
        pair = g // 2
        qa = _q_aug(qT_ref, g)

        def tile(t, carry, g=g, pair=pair, qa=qa):
            base = pl.multiple_of(t * WT, WT)
            sT = jnp.dot(k_ref[pl.ds(base, WT), pair * PAIR_W:(pair + 1) * PAIR_W], qa, preferred_element_type=F32)
            kpos = base + krow
            valid = (kpos <= qpos) & (kpos > qpos - WINDOW)
            vt = vT_ref[g * HEAD_DIM:(g + 1) * HEAD_DIM, pl.ds(base, WT)]
            return _online_softmax_step(sT, valid, *carry, vt)

        m, l, acc = lax.fori_loop(first, qb * (QB // WT) + QB // WT, tile, _softmax_init())
        _store_heads(o_ref, g, acc, 1.0 / jnp.maximum(l, 1e-30), gT_ref, 2)


def prompt_win_attention(qT, gT, kw, vwT, n_q):
    col = lambda i: (0, i)
    return pl.pallas_call(
        _prompt_win_body,
        out_shape=jax.ShapeDtypeStruct((NSA_WIDTH, n_q), F32),
        grid=(n_q // QB,),
        in_specs=[pl.BlockSpec((NSA_WIDTH, QB), col), pl.BlockSpec((LANES, QB), col),
                  pl.BlockSpec((n_q, KV_WIDTH), lambda i: (0, 0)),
                  pl.BlockSpec((KV_WIDTH, n_q), lambda i: (0, 0))],
        out_specs=pl.BlockSpec((NSA_WIDTH, QB), col),
        compiler_params=_cparams(("parallel",)),
        name="prompt_win_attention",
    )(qT, gT, kw, vwT)


def _nt_dot(a, b):
    return lax.dot_general(a, b, (((1,), (1,)), ((), ())), preferred_element_type=F32)


def _rows_softmax_step(s, valid, m, l, acc, v):
    s = jnp.where(valid, s, MASKED)
    m_new = jnp.maximum(m, jnp.max(s, axis=1, keepdims=True))
    alpha = jnp.exp(m - m_new)
    e = jnp.where(valid, jnp.exp(s - m_new), 0.0)
    l = l * alpha + jnp.sum(e, axis=1, keepdims=True)
    acc = acc * alpha + jnp.dot(e.astype(BF16), v, preferred_element_type=F32)
    return m_new, l, acc


def _sample_attn_body(pt_ref, *refs, past_len, n_tok, n_select):
    del pt_ref
    pages = refs[:PAGES_PER_STEP]
    (qa_ref, gate_ref, kc_ref, ov_ref, ksn_ref, win_ref, kwn_ref, o_ref,
     pick_ref, ocmp_ref, m_ref, l_ref, acc_ref) = refs[PAGES_PER_STEP:]
    st = pl.program_id(1)
    n_rows = N_HEADS * n_tok
    n_sel = ov_ref.shape[1]
    keys_per_step = PAGES_PER_STEP * PAGE
    qa = qa_ref[0]
    tok = lax.broadcasted_iota(I32, (n_rows, 1), 0) % n_tok
    qpos = past_len + tok

    @pl.when(st == 0)
    def _():
        n_cmp = kc_ref.shape[1]
        kc = kc_ref[0]
        m_idx = lax.broadcasted_iota(I32, (1, n_cmp), 1)
        usable = (m_idx >= 1) & (m_idx * CMP_STRIDE + (CMP_BLOCK - CMP_STRIDE - 1) <= qpos)
        s = jnp.where(usable, _nt_dot(qa, kc[:, :KV_WIDTH]), -jnp.inf)
        mx = jnp.max(s, axis=1, keepdims=True)
        mx = jnp.where(mx == -jnp.inf, 0.0, mx)
        e = jnp.exp(s - mx)
        pb = (e * (1.0 / jnp.maximum(jnp.sum(e, axis=1, keepdims=True), 1e-30))).astype(BF16)
        ocmp_ref[...] = jnp.dot(pb, kc[:, KV_WIDTH:], preferred_element_type=F32) * gate_ref[0, :, 0:1]
        imp_rows = jnp.dot(pb, ov_ref[...], preferred_element_type=F32)
        imp = imp_rows.reshape(N_GROUPS, HEADS_PER_GROUP, n_tok, n_sel).sum(axis=1).reshape(N_GROUPS * n_tok, n_sel)
        blk = lax.broadcasted_iota(I32, (N_GROUPS * n_tok, n_sel), 1)
        qp = past_len + lax.broadcasted_iota(I32, (N_GROUPS * n_tok, 1), 0) % n_tok
        dist = qp // SEL_BLOCK - blk
        forced = (blk == 0) | ((dist >= 0) & (dist < N_LOCAL_FORCED))
        adj = jnp.where(forced, POS_BIG, imp)

        def pick(_, carry):
            adj, sel = carry
            mx = jnp.max(adj, axis=1, keepdims=True)
            first = jnp.min(jnp.where(adj == mx, blk, n_sel), axis=1, keepdims=True)
            hit = blk == first
            return jnp.where(hit, -jnp.inf, adj), jnp.where(hit, 1.0, sel)

        _, sel = lax.fori_loop(0, n_select, pick, (adj, jnp.zeros((N_GROUPS * n_tok, n_sel), F32)))
        sel = jnp.broadcast_to(sel.reshape(N_GROUPS, 1, n_tok, n_sel), (N_GROUPS, HEADS_PER_GROUP, n_tok, n_sel))
        pick_ref[...] = sel.reshape(n_rows, n_sel).astype(BF16)
        m_ref[...] = jnp.full((n_rows, 1), MASKED, F32)
        l_ref[...] = jnp.zeros((n_rows, 1), F32)
        acc_ref[...] = jnp.zeros((n_rows, KV_WIDTH), F32)

    kv = jnp.concatenate([p[0] for p in pages], axis=0).astype(BF16)
    s = _nt_dot(qa, kv[:, :KV_WIDTH])
    sel_row = lax.broadcasted_iota(I32, (n_sel, keys_per_step), 0)
    key_blk = st * (keys_per_step // SEL_BLOCK) + lax.broadcasted_iota(I32, (n_sel, keys_per_step), 1) // SEL_BLOCK
    expand = jnp.where(sel_row == key_blk, 1.0, 0.0).astype(BF16)
    valid = jnp.dot(pick_ref[...], expand, preferred_element_type=F32) > 0.5
    m, l, acc = _rows_softmax_step(s, valid, m_ref[...], l_ref[...], acc_ref[...], kv[:, KV_WIDTH:])
    m_ref[...] = m
    l_ref[...] = l
    acc_ref[...] = acc

    @pl.when(st == pl.num_programs(1) - 1)
    def _():
        newer = lax.broadcasted_iota(I32, (1, n_tok), 1) <= tok
        ksn = ksn_ref[0].astype(BF16)
        m2, l2, acc2 = _rows_softmax_step(_nt_dot(qa, ksn[:, :KV_WIDTH]), newer, m, l, acc, ksn[:, KV_WIDTH:])
        o = ocmp_ref[...] + acc2 * (gate_ref[0, :, 1:2] / jnp.maximum(l2, 1e-30))
        n_buf = win_ref.shape[1]
        win = win_ref[0].astype(BF16)
        wpos = past_len - n_buf + lax.broadcasted_iota(I32, (1, n_buf), 1)
        wvalid = (wpos <= qpos) & (wpos > qpos - WINDOW) & (wpos >= 0)
        init = (jnp.full((n_rows, 1), MASKED, F32), jnp.zeros((n_rows, 1), F32), jnp.zeros((n_rows, KV_WIDTH), F32))
        st_w = _rows_softmax_step(_nt_dot(qa, win[:, :KV_WIDTH]), wvalid, *init, win[:, KV_WIDTH:])
        kwn = kwn_ref[0].astype(BF16)
        _, l3, acc3 = _rows_softmax_step(_nt_dot(qa, kwn[:, :KV_WIDTH]), newer, *st_w, kwn[:, KV_WIDTH:])
        o = o + acc3 * (gate_ref[0, :, 2:3] / jnp.maximum(l3, 1e-30))
        for h in range(N_HEADS):
            g = h // HEADS_PER_GROUP
            o_ref[0, :, h * HEAD_DIM:(h + 1) * HEAD_DIM] = o[h * n_tok:(h + 1) * n_tok, g * HEAD_DIM:(g + 1) * HEAD_DIM]


def _sample_q_layout(qT_s, gT_s, B, n_tok):
    q = qT_s.T.reshape(B, n_tok, N_HEADS, HEAD_DIM).transpose(0, 2, 1, 3)
    own = (jnp.arange(N_HEADS)[:, None] // HEADS_PER_GROUP == jnp.arange(N_GROUPS)[None, :]).astype(q.dtype)
    qa = (q[:, :, :, None, :] * own[None, :, None, :, None]).reshape(B, N_HEADS * n_tok, KV_WIDTH)
    g = gT_s[:N_HEADS * N_BRANCH].reshape(N_HEADS, N_BRANCH, B, n_tok).transpose(2, 0, 3, 1)
    g = jnp.pad(g.reshape(B, N_HEADS * n_tok, N_BRANCH), ((0, 0), (0, 0), (0, LANES - N_BRANCH)))
    return qa, g


def sample_attention(qa, gates, kc, cache_sel, table, ks_new, win_buf, kw_new, past_len):
    B, n_pages = table.shape
    n_tok = ks_new.shape[1]
    n_rows = N_HEADS * n_tok
    n_cmp = kc.shape[1]
    n_buf = win_buf.shape[1]
    assert n_pages % PAGES_PER_STEP == 0 and past_len == n_pages * PAGE and past_len % SEL_BLOCK == 0 and n_tok <= SEL_BLOCK
    n_sel = past_len // SEL_BLOCK
    n_steps = n_pages // PAGES_PER_STEP
    n_select = min(N_SELECT, n_sel + 1) - 1
    ov = _overlap_T(n_cmp, n_sel).T
    flat = table.reshape(-1).astype(I32)

    def page_map(k):
        return lambda b, st, pt: (pt[b * n_pages + st * PAGES_PER_STEP + k], 0, 0)

    per_b = lambda b, st, pt: (b, 0, 0)
    in_specs = [pl.BlockSpec((1, PAGE, KV_ROW), page_map(k)) for k in range(PAGES_PER_STEP)]
    in_specs += [pl.BlockSpec((1, n_rows, KV_WIDTH), per_b), pl.BlockSpec((1, n_rows, LANES), per_b),
                 pl.BlockSpec((1, n_cmp, KV_ROW), per_b), pl.BlockSpec((n_cmp, n_sel), lambda b, st, pt: (0, 0)),
                 pl.BlockSpec((1, n_tok, KV_ROW), per_b), pl.BlockSpec((1, n_buf, KV_ROW), per_b),
                 pl.BlockSpec((1, n_tok, KV_ROW), per_b)]
    grid_spec = pltpu.PrefetchScalarGridSpec(
        num_scalar_prefetch=1,
        grid=(B, n_steps),
        in_specs=in_specs,
        out_specs=pl.BlockSpec((1, n_tok, NSA_WIDTH), per_b),
        scratch_shapes=[pltpu.VMEM((n_rows, n_sel), BF16), pltpu.VMEM((n_rows, KV_WIDTH), F32),
                        pltpu.VMEM((n_rows, 1), F32), pltpu.VMEM((n_rows, 1), F32),
                        pltpu.VMEM((n_rows, KV_WIDTH), F32)],
    )
    return pl.pallas_call(
        functools.partial(_sample_attn_body, past_len=past_len, n_tok=n_tok, n_select=n_select),
        out_shape=jax.ShapeDtypeStruct((B, n_tok, NSA_WIDTH), F32),
        grid_spec=grid_spec,
        compiler_params=_cparams(("parallel", "arbitrary")),
        name="sample_attention",
    )(flat, *([cache_sel] * PAGES_PER_STEP), qa, gates, kc, ov, ks_new, win_buf, kw_new)


HALO = 16


def _pool_body(halo_ref, cur_ref, w_ref, b_ref, scale_ref, o_ref, ctx_ref, *, n_prev, seq_tiles, halo_is_self):
    i = pl.program_id(0)
    tm = cur_ref.shape[0]
    cur = cur_ref[...]
    halo = halo_ref[...]
    if halo_is_self:
        halo = jnp.where(i % seq_tiles == 0, 0.0, halo)
    ctx_ref[0:HALO, :] = halo
    ctx_ref[HALO:HALO + tm, :] = cur
    t_abs = (i % seq_tiles) * tm + lax.broadcasted_iota(I32, (tm, 1), 0)
    outs = []
    for g, w in enumerate(POOL_WINDOWS):
        lo, hi = g * POOL_GROUP_IN, (g + 1) * POOL_GROUP_IN
        acc = cur[:, lo:hi]
        for k in range(1, w):
            acc = acc + ctx_ref[HALO - k:HALO - k + tm, lo:hi]
        cnt = jnp.minimum(w, t_abs + 1 + n_prev).astype(F32)
        pooled = (acc / cnt - cur[:, lo:hi]).astype(BF16)
        outs.append(jnp.dot(pooled, w_ref[g], preferred_element_type=F32) + b_ref[g:g + 1, :])
    o_ref[...] = jnp.concatenate(outs, axis=1) * scale_ref[...]


def pool_mixer(u, halo_src, pool_w, pool_b, pool_scale, *, tm, row0, n_rows, n_prev, halo_is_self):
    n_tiles = n_rows // tm
    assert n_rows % tm == 0 and row0 % tm == 0 and tm % SUBLANES == 0
    if halo_is_self:
        assert tm % HALO == 0
        halo_map = lambda i: (jnp.maximum((row0 + i * tm) // HALO - 1, 0), 0)
        seq_tiles = n_tiles
    else:
        halo_map = lambda i: (i, 0)
        seq_tiles = 1
    return pl.pallas_call(
        functools.partial(_pool_body, n_prev=n_prev, seq_tiles=seq_tiles, halo_is_self=halo_is_self),
        out_shape=jax.ShapeDtypeStruct((n_rows, D_MODEL), F32),
        grid=(n_tiles,),
        in_specs=[pl.BlockSpec((HALO, POOL_WIDTH), halo_map),
                  pl.BlockSpec((tm, POOL_WIDTH), lambda i: (row0 // tm + i, 0)),
                  pl.BlockSpec((POOL_GROUPS, POOL_GROUP_IN, POOL_GROUP_OUT), lambda i: (0, 0, 0)),
                  pl.BlockSpec((POOL_GROUPS, POOL_GROUP_OUT), lambda i: (0, 0)),
                  pl.BlockSpec((1, D_MODEL), lambda i: (0, 0))],
        out_specs=pl.BlockSpec((tm, D_MODEL), lambda i: (i, 0)),
        scratch_shapes=[pltpu.VMEM((HALO + tm, POOL_WIDTH), F32)],
        compiler_params=_cparams(("arbitrary",)),
        name="pool_mixer",
    )(halo_src, u, pool_w, pool_b, pool_scale)


def _layer_norm(v, g, b):
    mu = jnp.mean(v, axis=-1, keepdims=True)
    c = v - mu
    var = jnp.mean(c * c, axis=-1, keepdims=True)
    return c * lax.rsqrt(var + LN_EPS) * g + b


def _merge_body(oc_ref, os_ref, ow_ref, osamp_ref, pbp_ref, pbs_ref, bga_ref, bgb_ref, x_ref, wa_ref, wo_ref,
                g_ref, b_ref, rw_ref, rb_ref, h_ref, te_ref, tg_ref, o_sc, pb_sc, *, n_prompt_tiles, alpha):
    i = pl.program_id(0)

    @pl.when(i < n_prompt_tiles)
    def _():
        o_sc[...] = (oc_ref[...] + os_ref[...] + ow_ref[...]).T.astype(BF16)
        pb_sc[...] = pbp_ref[...]

    @pl.when(i >= n_prompt_tiles)
    def _():
        o_sc[...] = osamp_ref[...].astype(BF16)
        pb_sc[...] = pbs_ref[...]

    a = jnp.dot(o_sc[...], wa_ref[...], preferred_element_type=F32)
    mixed = (bga_ref[...] * a + bgb_ref[...] * pb_sc[...]).astype(BF16)
    mix = jnp.dot(mixed, wo_ref[...], preferred_element_type=F32)
    h = _layer_norm(alpha * x_ref[...] + mix, g_ref[...], b_ref[...])
    h_ref[...] = h
    logits = jnp.dot(h.astype(BF16), rw_ref[...], preferred_element_type=F32) + rb_ref[...]
    lane = lax.broadcasted_iota(I32, logits.shape, 1)
    picks_e = jnp.zeros(logits.shape, I32)
    picks_l = jnp.full(logits.shape, -jnp.inf, F32)
    for k in range(TOP_K):
        mx = jnp.max(logits, axis=1, keepdims=True)
        idx = jnp.min(jnp.where(logits == mx, lane, LANES), axis=1, keepdims=True)
        picks_e = jnp.where(lane == k, idx, picks_e)
        picks_l = jnp.where(lane == k, mx, picks_l)
        logits = jnp.where(lane == idx, -jnp.inf, logits)
    ex = jnp.exp(picks_l - jnp.max(picks_l, axis=1, keepdims=True))
    te_ref[...] = picks_e
    tg_ref[...] = ex / jnp.sum(ex, axis=1, keepdims=True)


def merge_norm_route(ocT, osT, owT, o_samp, pb_p, pb_s, bg, x_all, wa, wo, ln_g, ln_b, rw, rb, *, tm, n_prompt, alpha):
    T = x_all.shape[0]
    n_pt = n_prompt // tm
    n_tiles = T // tm
    assert n_prompt % tm == 0 and T % tm == 0 and o_samp.shape[0] == T - n_prompt
    pcol = lambda i: (0, jnp.minimum(i, n_pt - 1))
    prow = lambda i: (jnp.minimum(i, n_pt - 1), 0)
    srow = lambda i: (jnp.maximum(i - n_pt, 0), 0)
    row = lambda i: (i, 0)
    const = lambda i: (0, 0)
    return pl.pallas_call(
        functools.partial(_merge_body, n_prompt_tiles=n_pt, alpha=alpha),
        out_shape=(jax.ShapeDtypeStruct((T, D_MODEL), F32), jax.ShapeDtypeStruct((T, LANES), I32),
                   jax.ShapeDtypeStruct((T, LANES), F32)),
        grid=(n_tiles,),
        in_specs=[pl.BlockSpec((NSA_WIDTH, tm), pcol), pl.BlockSpec((NSA_WIDTH, tm), pcol),
                  pl.BlockSpec((NSA_WIDTH, tm), pcol), pl.BlockSpec((tm, NSA_WIDTH), srow),
                  pl.BlockSpec((tm, D_MODEL), prow), pl.BlockSpec((tm, D_MODEL), srow),
                  pl.BlockSpec((tm, D_MODEL), lambda i: (i, 0)), pl.BlockSpec((tm, D_MODEL), lambda i: (i, 1)),
                  pl.BlockSpec((tm, D_MODEL), row),
                  pl.BlockSpec((NSA_WIDTH, D_MODEL), const), pl.BlockSpec((D_MODEL, D_MODEL), const),
                  pl.BlockSpec((1, D_MODEL), const), pl.BlockSpec((1, D_MODEL), const),
                  pl.BlockSpec((D_MODEL, LANES), const), pl.BlockSpec((1, LANES), const)],
        out_specs=(pl.BlockSpec((tm, D_MODEL), row), pl.BlockSpec((tm, LANES), row), pl.BlockSpec((tm, LANES), row)),
        scratch_shapes=[pltpu.VMEM((tm, NSA_WIDTH), BF16), pltpu.VMEM((tm, D_MODEL), F32)],
        compiler_params=_cparams(("parallel",)),
        name="merge_norm_route",
    )(ocT, osT, owT, o_samp, pb_p, pb_s, bg, bg, x_all, wa, wo, ln_g, ln_b, rw, rb)


MOE_ROWS = 1024
FF_TILE = 256


def _moe_body(be_ref, na_ref, tok_ref, h_hbm, gate_ref, wg_ref, wu_ref, bg_ref, bu_ref, wd_ref, bd_ref,
              o_ref, xg_ref, xb_ref, sem):
    del be_ref
    blk = pl.program_id(0)
    f = pl.program_id(1)
    active = blk < na_ref[0]

    def row_copy(r, src_row):
        return pltpu.make_async_copy(h_hbm.at[pl.ds(src_row, 1), :], xg_ref.at[pl.ds(r, 1), :], sem)

    @pl.when(active & (f == 0))
    def _():
        def issue(r, c):
            row_copy(r, tok_ref[0, 0, r]).start()
            return c

        def drain(r, c):
            row_copy(r, 0).wait()
            return c

        lax.fori_loop(0, MOE_ROWS, issue, 0)
        lax.fori_loop(0, MOE_ROWS, drain, 0)
        xb_ref[...] = xg_ref[...].astype(BF16)
        o_ref[...] = jnp.zeros_like(o_ref)

    @pl.when(active)
    def _():
        xb = xb_ref[...]
        hg = jnp.dot(xb, wg_ref[0].astype(BF16), preferred_element_type=F32) + bg_ref[0]
        hu = jnp.dot(xb, wu_ref[0].astype(BF16), preferred_element_type=F32) + bu_ref[0]
        gt = jnp.minimum(hg, SWIGLU_LIMIT)
        up = jnp.clip(hu, -SWIGLU_LIMIT, SWIGLU_LIMIT)
        act = (up + 1.0) * gt * jax.nn.sigmoid(SWIGLU_ALPHA * gt)
        o_ref[...] += jnp.dot(act.astype(BF16), wd_ref[0].astype(BF16), preferred_element_type=F32)

    @pl.when(active & (f == pl.num_programs(1) - 1))
    def _():
        o_ref[...] = (o_ref[...] + bd_ref[0]) * gate_ref[...]

    @pl.when(jnp.logical_not(active) & (f == 0))
    def _():
        o_ref[...] = jnp.zeros_like(o_ref)


def moe_experts(h, row_tok, row_gate, blk_expert, n_active, w_up, b_up, w_down, b_down):
    n_blocks = row_tok.shape[0]
    E, D, F2 = w_up.shape
    d_ff = F2 // 2
    n_f = d_ff // FF_TILE
    n_rows = n_blocks * MOE_ROWS

    def e_of(b, be, na):
        return be[jnp.minimum(b, na[0] - 1)]

    def f_of(b, f, na):
        return jnp.where(b < na[0], f, n_f - 1)

    b_up3 = b_up.reshape(E, 1, F2)
    b_down3 = b_down.reshape(E, 1, D)
    grid_spec = pltpu.PrefetchScalarGridSpec(
        num_scalar_prefetch=2,
        grid=(n_blocks, n_f),
        in_specs=[
            pl.BlockSpec((1, 1, MOE_ROWS), lambda b, f, be, na: (b, 0, 0), memory_space=pltpu.SMEM),
            pl.BlockSpec(memory_space=pl.ANY),
            pl.BlockSpec((MOE_ROWS, 1), lambda b, f, be, na: (b, 0)),
            pl.BlockSpec((1, D, FF_TILE), lambda b, f, be, na: (e_of(b, be, na), 0, f_of(b, f, na))),
            pl.BlockSpec((1, D, FF_TILE), lambda b, f, be, na: (e_of(b, be, na), 0, n_f + f_of(b, f, na))),
            pl.BlockSpec((1, 1, FF_TILE), lambda b, f, be, na: (e_of(b, be, na), 0, f_of(b, f, na))),
            pl.BlockSpec((1, 1, FF_TILE), lambda b, f, be, na: (e_of(b, be, na), 0, n_f + f_of(b, f, na))),
            pl.BlockSpec((1, FF_TILE, D), lambda b, f, be, na: (e_of(b, be, na), f_of(b, f, na), 0)),
            pl.BlockSpec((1, 1, D), lambda b, f, be, na: (e_of(b, be, na), 0, 0)),
        ],
        out_specs=pl.BlockSpec((MOE_ROWS, D), lambda b, f, be, na: (b, 0)),
        scratch_shapes=[pltpu.VMEM((MOE_ROWS, D), F32), pltpu.VMEM((MOE_ROWS, D), BF16), pltpu.SemaphoreType.DMA(())],
    )
    return pl.pallas_call(
        _moe_body,
        out_shape=jax.ShapeDtypeStruct((n_rows, D), F32),
        grid_spec=grid_spec,
        compiler_params=_cparams(("arbitrary", "arbitrary")),
        name="moe_experts",
    )(blk_expert, n_active, row_tok.reshape(n_blocks, 1, MOE_ROWS), h, row_gate, w_up, w_up, b_up3, b_up3, w_down,
      b_down3)


def _moe_dispatch(top_e, top_g, n_experts):
    T = top_e.shape[0]
    n_pairs = T * TOP_K
    e_flat = top_e.reshape(-1)
    order = jnp.argsort(e_flat)
    e_sorted = e_flat[order]
    counts = jnp.bincount(e_flat, length=n_experts)
    padded = (counts + MOE_ROWS - 1) // MOE_ROWS * MOE_ROWS
    pad_end = jnp.cumsum(padded)
    pad_start = pad_end - padded
    start = jnp.cumsum(counts) - counts
    dest = pad_start[e_sorted] + (jnp.arange(n_pairs) - start[e_sorted])
    n_blocks = -(-n_pairs // MOE_ROWS) + n_experts
    n_rows = n_blocks * MOE_ROWS
    tok_sorted = (order // TOP_K).astype(I32)
    row_tok = jnp.zeros((n_rows,), I32).at[dest].set(tok_sorted)
    row_gate = jnp.zeros((n_rows,), F32).at[dest].set(top_g.reshape(-1)[order])
    blk_expert = jnp.clip(jnp.searchsorted(pad_end, jnp.arange(n_blocks) * MOE_ROWS, side='right'), 0, n_experts - 1)
    n_active = (pad_end[-1] // MOE_ROWS).astype(I32).reshape(1)
    pair_row = jnp.zeros((n_pairs,), I32).at[order].set(dest.astype(I32))
    return (row_tok.reshape(n_blocks, MOE_ROWS), row_gate.reshape(n_rows, 1), blk_expert.astype(I32), n_active,
            pair_row.reshape(T, TOP_K))


def _combine_body(pos_ref, rows_hbm, h_ref, g_ref, b_ref, y_ref, buf_ref, sem, *, alpha):
    tm = h_ref.shape[0]

    def row_copy(t, k, src_row):
        return pltpu.make_async_copy(rows_hbm.at[pl.ds(src_row, 1), :], buf_ref.at[k, pl.ds(t, 1), :], sem)

    def issue(t, c):
        for k in range(TOP_K):
            row_copy(t, k, pos_ref[0, 0, t * TOP_K + k]).start()
        return c

    def drain(t, c):
        for k in range(TOP_K):
            row_copy(t, k, 0).wait()
        return c

    lax.fori_loop(0, tm, issue, 0)
    lax.fori_loop(0, tm, drain, 0)
    moe = buf_ref[0]
    for k in range(1, TOP_K):
        moe = moe + buf_ref[k]
    h = h_ref[...]
    y_ref[...] = _layer_norm(alpha * h + moe, g_ref[...], b_ref[...])


def combine_norm(out_rows, pair_row, h, ln_g, ln_b, *, tm, alpha):
    T, D = h.shape
    assert T % tm == 0
    pos = pair_row.reshape(T // tm, 1, tm * TOP_K)
    return pl.pallas_call(
        functools.partial(_combine_body, alpha=alpha),
        out_shape=jax.ShapeDtypeStruct((T, D), F32),
        grid=(T // tm,),
        in_specs=[pl.BlockSpec((1, 1, tm * TOP_K), lambda i: (i, 0, 0), memory_space=pltpu.SMEM),
                  pl.BlockSpec(memory_space=pl.ANY),
                  pl.BlockSpec((tm, D), lambda i: (i, 0)),
                  pl.BlockSpec((1, D), lambda i: (0, 0)), pl.BlockSpec((1, D), lambda i: (0, 0))],
        out_specs=pl.BlockSpec((tm, D), lambda i: (i, 0)),
        scratch_shapes=[pltpu.VMEM((TOP_K, tm, D), F32), pltpu.SemaphoreType.DMA(())],
        compiler_params=_cparams(("arbitrary",)),
        name="combine_norm",
    )(pos, out_rows, h, ln_g, ln_b)


def _largest_tile(n, candidates):
    for c in candidates:
        if n % c == 0:
            return c
    raise ValueError(f"no tile of {candidates} divides {n}")


def _layer(xp, xs, cache_cmp, cache_sel, win_buf, pool_buf, table, w_in, cmp_pos, cmp_w, w_branch_a, pool_w, pool_b,
           pool_scale, w_out, ln1_g, ln1_b, router_w, router_b, w_up, b_up, w_down, b_down, ln2_g, ln2_b, alpha):
    TP = xp.shape[0]
    B, n_tok, _ = xs.shape
    TS = B * n_tok
    T = TP + TS
    n_experts = router_w.shape[1]
    n_pages = table.shape[1]
    past_len = n_pages * PAGE
    assert T % PAGE == 0 and TP % (PAGE * PAGES_PER_STEP) == 0 and TP >= WINDOW and TP >= POOL_BUF

    x_all = jnp.concatenate([xp, xs.reshape(TS, D_MODEL)], axis=0)
    (qT, kvc, kvs, ks, vsT, kvw, kw, vwT, gT, u, bg) = in_projection(
        x_all, _prep_w_in(w_in), _largest_tile(T, (640, 512, 384, 256, 128)))
    pos_rows, bdk, bdv = _prep_cmp(cmp_pos, cmp_w)

    kc_p, vcT_p = compress_kv(kvc.reshape(T // PAGE, PAGE, KV_ROW), jnp.arange(TP // PAGE, dtype=I32)[None],
                              pos_rows, bdk, bdv)
    ocT, selT = prompt_cmp_select(qT, gT, kc_p, vcT_p, TP)
    osT = prompt_sel_attention(qT, gT, selT, ks, vsT, TP)
    owT = prompt_win_attention(qT, gT, kw, vwT, TP)

    kc_s, _ = compress_kv(cache_cmp, table, pos_rows, bdk, bdv)
    qa, gates = _sample_q_layout(qT[:, TP:], gT[:, TP:], B, n_tok)
    ks_new = kvs[TP:].reshape(B, n_tok, KV_ROW)
    kw_new = kvw[TP:].reshape(B, n_tok, KV_ROW)
    o_s = sample_attention(qa, gates, kc_s, cache_sel, table, ks_new, win_buf, kw_new, past_len).reshape(TS, NSA_WIDTH)

    pw = pool_w.astype(BF16)
    scale = pool_scale.reshape(1, D_MODEL)
    pb_p = pool_mixer(u, u, pw, pool_b, scale, tm=_largest_tile(TP, (512, 256, 128)), row0=0, n_rows=TP, n_prev=0,
                      halo_is_self=True)
    halo_s = jnp.concatenate([jnp.zeros((B, HALO - POOL_BUF, POOL_WIDTH), F32), pool_buf], axis=1)
    pb_s = pool_mixer(u, halo_s.reshape(B * HALO, POOL_WIDTH), pw, pool_b, scale, tm=n_tok, row0=TP, n_rows=TS,
                      n_prev=POOL_BUF, halo_is_self=False)

    rw = jnp.pad(router_w, ((0, 0), (0, LANES - n_experts))).astype(BF16)
    rb = jnp.pad(router_b, (0, LANES - n_experts), constant_values=-jnp.inf).reshape(1, LANES)
    tm_tok = _largest_tile(TS, (256, 128))
    h, top_e, top_g = merge_norm_route(
        ocT, osT, owT, o_s, pb_p, pb_s, bg, x_all, w_branch_a.astype(BF16), w_out.astype(BF16),
        ln1_g.reshape(1, D_MODEL), ln1_b.reshape(1, D_MODEL), rw, rb, tm=tm_tok, n_prompt=TP, alpha=alpha)
    row_tok, row_gate, blk_expert, n_active, pair_row = _moe_dispatch(top_e[:, :TOP_K], top_g[:, :TOP_K], n_experts)
    out_rows = moe_experts(h, row_tok, row_gate, blk_expert, n_active, w_up, b_up, w_down, b_down)
    y = combine_norm(out_rows, pair_row, h, ln2_g.reshape(1, D_MODEL), ln2_b.reshape(1, D_MODEL), tm=tm_tok, alpha=alpha)

    kv5 = lambda a, n: a.reshape(n, 2, N_GROUPS, HEAD_DIM)
    u_s = u[TP:].reshape(B, n_tok, POOL_WIDTH)
    return dict(
        y_p=y[:TP], y_s=y[TP:].reshape(B, n_tok, D_MODEL),
        kvc_p=kv5(kvc[:TP], TP), kvc_s=kv5(kvc[TP:], TS).reshape(B, n_tok, 2, N_GROUPS, HEAD_DIM),
        kvs_p=kv5(kvs[:TP], TP), kvs_s=kv5(kvs[TP:], TS).reshape(B, n_tok, 2, N_GROUPS, HEAD_DIM),
        win_p=kv5(kvw[TP - WINDOW:TP], WINDOW),
        win_s=jnp.concatenate([win_buf, kw_new], axis=1)[:, n_tok:].reshape(B, -1, 2, N_GROUPS, HEAD_DIM),
        pool_p=u[TP - POOL_BUF:TP],
        pool_s=jnp.concatenate([pool_buf, u_s], axis=1)[:, -POOL_BUF:],
    )


def kernel(x_prompt, x_sample, cache_cmp_kv, cache_sel_kv, state_win_kv, state_pool, page_table, w_in, cmp_pos, cmp_w, w_branch_a, pool_w, pool_b, pool_scale, w_out, ln1_g, ln1_b, router_w, router_b, w_up, b_up, w_down, b_down, ln2_g, ln2_b):
    depth = w_in.shape[0]
    assert x_prompt.shape[0] == 1, "one prompt sequence"
    alpha = (2.0 * depth) ** 0.25
    n_phys = cache_cmp_kv.shape[1]
    B = x_sample.shape[0]
    xp, xs = x_prompt[0], x_sample
    per_layer = []
    for l in range(depth):
        r = _layer(xp, xs, cache_cmp_kv[l].reshape(n_phys, PAGE, KV_ROW), cache_sel_kv[l].reshape(n_phys, PAGE, KV_ROW),
                   state_win_kv[l].reshape(B, -1, KV_ROW), state_pool[l], page_table, w_in[l], cmp_pos[l], cmp_w[l],
                   w_branch_a[l], pool_w[l], pool_b[l], pool_scale[l], w_out[l], ln1_g[l], ln1_b[l], router_w[l],
                   router_b[l], w_up[l], b_up[l], w_down[l], b_down[l], ln2_g[l], ln2_b[l], alpha)
        xp, xs = r["y_p"], r["y_s"]
        per_layer.append(r)
    stack = lambda k, lead: jnp.stack([r[k][None] if lead else r[k] for r in per_layer])
    return (xp[None], xs, stack("kvc_p", True), stack("kvc_s", False), stack("kvs_p", True), stack("kvs_s", False),
            stack("win_p", True), stack("win_s", False), stack("pool_p", True), stack("pool_s", False))
```

```python
import functools

import jax
import jax.numpy as jnp
import numpy as np
from jax import lax
from jax.experimental import pallas as pl
from jax.experimental.pallas import tpu as pltpu

F32 = jnp.float32
BF16 = jnp.bfloat16
I32 = jnp.int32

D_MODEL = 2048
N_HEADS = 16
N_GROUPS = 4
HEAD_DIM = 64
HEADS_PER_GROUP = N_HEADS // N_GROUPS
NSA_WIDTH = N_HEADS * HEAD_DIM
KV_WIDTH = N_GROUPS * HEAD_DIM
KV_ROW = 2 * KV_WIDTH
N_BRANCH = 3
CMP_BLOCK = 32
CMP_STRIDE = 16
SEL_BLOCK = 64
N_SELECT = 16
N_LOCAL_FORCED = 2
WINDOW = 512
POOL_WINDOWS = (2, 4, 8, 16)
POOL_GROUPS = 4
POOL_WIDTH = D_MODEL // 2
POOL_GROUP_IN = POOL_WIDTH // POOL_GROUPS
POOL_GROUP_OUT = D_MODEL // POOL_GROUPS
POOL_BUF = 15
N_MIXERS = 2
TOP_K = 4
SWIGLU_LIMIT = 7.0
SWIGLU_ALPHA = 1.702
LN_EPS = 1e-5
POS_BIG = 1e30
NEG_BIG = -1e30

LANES = 128
SUBLANES = 8
VMEM_LIMIT = 56 * 1024 * 1024

PAGE = 128
QB = 128
CHUNKS_PER_PAGE = PAGE // CMP_STRIDE


def _cparams(sem, vmem=VMEM_LIMIT):
    return pltpu.CompilerParams(dimension_semantics=sem, vmem_limit_bytes=vmem)


SEG = 512
_SEG_Q, _SEG_KVC, _SEG_KVS, _SEG_KVW, _SEG_NG, _SEG_U, _SEG_BG = 0, 2, 3, 4, 5, 6, 8
N_SEG = 16


def _inproj_body(x_ref, w_ref, qT_ref, kvc_ref, kvs_ref, ks_ref, vsT_ref, kvw_ref, kw_ref, vwT_ref,
                 gT_ref, u_ref, bg_ref, xb_ref):
    j = pl.program_id(1)

    @pl.when(j == 0)
    def _():
        xb_ref[...] = x_ref[...].astype(BF16)

    z = jnp.dot(xb_ref[...], w_ref[...], preferred_element_type=F32)

    @pl.when(j < _SEG_KVC)
    def _():
        qT_ref[...] = (z * (HEAD_DIM ** -0.5)).T.astype(BF16)

    @pl.when(j == _SEG_KVC)
    def _():
        kvc_ref[...] = z.T

    @pl.when(j == _SEG_KVS)
    def _():
        zT = z.T
        kvs_ref[...] = zT
        ks_ref[...] = z[:, :KV_WIDTH].astype(BF16)
        vsT_ref[...] = zT[KV_WIDTH:, :].astype(BF16)

    @pl.when(j == _SEG_KVW)
    def _():
        zT = z.T
        kvw_ref[...] = zT
        kw_ref[...] = z[:, :KV_WIDTH].astype(BF16)
        vwT_ref[...] = zT[KV_WIDTH:, :].astype(BF16)

    @pl.when(j == _SEG_NG)
    def _():
        gT_ref[...] = jax.nn.sigmoid(z[:, :LANES]).T

    @pl.when((j >= _SEG_U) & (j < _SEG_BG))
    def _():
        u_ref[...] = z

    @pl.when(j >= _SEG_BG)
    def _():
        bg_ref[...] = jax.nn.sigmoid(z)


def _prep_w_in(w_in):
    sizes = (NSA_WIDTH, KV_ROW, KV_ROW, KV_ROW, N_HEADS * N_BRANCH, POOL_WIDTH, N_MIXERS * D_MODEL)
    offs = np.concatenate([[0], np.cumsum(sizes)])
    parts = []
    for k, s in enumerate(sizes):
        p = w_in[:, offs[k]:offs[k + 1]]
        pad = (-s) % SEG
        if pad:
            p = jnp.pad(p, ((0, 0), (0, pad)))
        parts.append(p)
    return jnp.concatenate(parts, axis=1).astype(BF16)


def in_projection(x, w_cat, tm):
    T, D = x.shape
    assert T % tm == 0 and tm % LANES == 0
    grid = (T // tm, N_SEG)
    row = lambda i, j: (i, 0)
    colT = lambda i, j: (0, i)
    out_shape = (
        jax.ShapeDtypeStruct((NSA_WIDTH, T), BF16),
        jax.ShapeDtypeStruct((KV_ROW, T), F32),
        jax.ShapeDtypeStruct((KV_ROW, T), F32),
        jax.ShapeDtypeStruct((T, KV_WIDTH), BF16),
        jax.ShapeDtypeStruct((KV_WIDTH, T), BF16),
        jax.ShapeDtypeStruct((KV_ROW, T), F32),
        jax.ShapeDtypeStruct((T, KV_WIDTH), BF16),
        jax.ShapeDtypeStruct((KV_WIDTH, T), BF16),
        jax.ShapeDtypeStruct((LANES, T), F32),
        jax.ShapeDtypeStruct((T, POOL_WIDTH), F32),
        jax.ShapeDtypeStruct((T, N_MIXERS * D_MODEL), F32),
    )
    out_specs = (
        pl.BlockSpec((SEG, tm), lambda i, j: (jnp.minimum(j, _SEG_KVC - 1), i)),
        pl.BlockSpec((KV_ROW, tm), colT),
        pl.BlockSpec((KV_ROW, tm), colT),
        pl.BlockSpec((tm, KV_WIDTH), row),
        pl.BlockSpec((KV_WIDTH, tm), colT),
        pl.BlockSpec((KV_ROW, tm), colT),
        pl.BlockSpec((tm, KV_WIDTH), row),
        pl.BlockSpec((KV_WIDTH, tm), colT),
        pl.BlockSpec((LANES, tm), colT),
        pl.BlockSpec((tm, SEG), lambda i, j: (i, jnp.clip(j - _SEG_U, 0, _SEG_BG - _SEG_U - 1))),
        pl.BlockSpec((tm, SEG), lambda i, j: (i, jnp.clip(j - _SEG_BG, 0, N_SEG - _SEG_BG - 1))),
    )
    return pl.pallas_call(
        _inproj_body,
        out_shape=out_shape,
        grid=grid,
        in_specs=[pl.BlockSpec((tm, D), row), pl.BlockSpec((D, SEG), lambda i, j: (0, j))],
        out_specs=out_specs,
        scratch_shapes=[pltpu.VMEM((tm, D), BF16)],
        compiler_params=_cparams(("parallel", "arbitrary")),
        name="in_projection",
    )(x, w_cat)


PAGES_PER_STEP = 16
_LT = KV_ROW // LANES


def _compress_body(pt_ref, *refs):
    del pt_ref
    pages = refs[:PAGES_PER_STEP]
    pos_ref, bdk_ref, bdv_ref, out_ref, vT_ref, sh_ref, rows_ref = refs[PAGES_PER_STEP:]
    st = pl.program_id(1)
    n = PAGES_PER_STEP * CHUNKS_PER_PAGE

    @pl.when(st == 0)
    def _():
        sh_ref[0:SUBLANES, :] = jnp.zeros((SUBLANES, KV_ROW), F32)

    for k, p in enumerate(pages):
        for c in range(_LT):
            rows_ref[c, k * PAGE:(k + 1) * PAGE, :] = p[0, c * LANES:(c + 1) * LANES, :].T

    acc = [jnp.zeros((n, KV_ROW), F32), jnp.zeros((n, KV_ROW), F32)]
    for j in range(CMP_STRIDE):
        xj = jnp.concatenate([rows_ref[c, pl.ds(j, n, stride=CMP_STRIDE), :] for c in range(_LT)], axis=1)
        for r in range(CMP_BLOCK // CMP_STRIDE):
            s = r * CMP_STRIDE + j
            lhs = (xj + pos_ref[s:s + 1, :]).astype(BF16)
            kk = jnp.dot(lhs[:, :KV_WIDTH], bdk_ref[s], preferred_element_type=F32)
            vv = jnp.dot(lhs[:, KV_WIDTH:], bdv_ref[s], preferred_element_type=F32)
            acc[r] = acc[r] + jnp.concatenate([kk, vv], axis=1)
    sh_ref[SUBLANES:SUBLANES + n, :] = acc[0]
    res = sh_ref[SUBLANES - 1:SUBLANES - 1 + n, :] + acc[1]
    sh_ref[SUBLANES - 1:SUBLANES, :] = acc[0][n - 1:n, :]
    out_ref[0] = res.astype(BF16)
    vT_ref[0] = res[:, KV_WIDTH:].T.astype(BF16)


def _prep_cmp(cmp_pos, cmp_w):
    pos_rows = jnp.concatenate([jnp.tile(cmp_pos[0], (1, N_GROUPS)), jnp.tile(cmp_pos[1], (1, N_GROUPS))], axis=1)
    eye = jnp.eye(N_GROUPS, dtype=cmp_w.dtype)
    bd = jnp.einsum('gh,csde->csgdhe', eye, cmp_w).reshape(2, CMP_BLOCK, KV_WIDTH, KV_WIDTH).astype(BF16)
    return pos_rows.astype(F32), bd[0], bd[1]


def compress_kv(src, table, pos_rows, bdk, bdv, *, pages_on_lanes):
    B, n_pages = table.shape
    assert n_pages % PAGES_PER_STEP == 0
    n_steps = n_pages // PAGES_PER_STEP
    n = PAGES_PER_STEP * CHUNKS_PER_PAGE
    flat = table.reshape(-1).astype(I32)

    def page_map(k):
        if pages_on_lanes:
            return lambda b, st, pt: (0, 0, pt[b * n_pages + st * PAGES_PER_STEP + k])
        return lambda b, st, pt: (pt[b * n_pages + st * PAGES_PER_STEP + k], 0, 0)

    const2 = lambda b, st, pt: (0, 0)
    const3 = lambda b, st, pt: (0, 0, 0)
    in_specs = [pl.BlockSpec((1, KV_ROW, PAGE), page_map(k)) for k in range(PAGES_PER_STEP)]
    in_specs += [pl.BlockSpec((CMP_BLOCK, KV_ROW), const2),
                 pl.BlockSpec((CMP_BLOCK, KV_WIDTH, KV_WIDTH), const3),
                 pl.BlockSpec((CMP_BLOCK, KV_WIDTH, KV_WIDTH), const3)]
    grid_spec = pltpu.PrefetchScalarGridSpec(
        num_scalar_prefetch=1,
        grid=(B, n_steps),
        in_specs=in_specs,
        out_specs=(pl.BlockSpec((1, n, KV_ROW), lambda b, st, pt: (b, st, 0)),
                   pl.BlockSpec((1, KV_WIDTH, n), lambda b, st, pt: (b, 0, st))),
        scratch_shapes=[pltpu.VMEM((n + 2 * SUBLANES, KV_ROW), F32),
                        pltpu.VMEM((_LT, PAGES_PER_STEP * PAGE, LANES), F32)],
    )
    return pl.pallas_call(
        _compress_body,
        out_shape=(jax.ShapeDtypeStruct((B, n_pages * CHUNKS_PER_PAGE, KV_ROW), BF16),
                   jax.ShapeDtypeStruct((B, KV_WIDTH, n_pages * CHUNKS_PER_PAGE), BF16)),
        grid_spec=grid_spec,
        compiler_params=_cparams(("parallel", "arbitrary")),
        name="compress_kv",
    )(flat, *([src] * PAGES_PER_STEP), pos_rows, bdk, bdv)


GQ = HEADS_PER_GROUP * QB
PAIR_W = 2 * HEAD_DIM


def _q_aug(qT_ref, g):
    qg = jnp.concatenate([qT_ref[(HEADS_PER_GROUP * g + r) * HEAD_DIM:(HEADS_PER_GROUP * g + r + 1) * HEAD_DIM, :]
                          for r in range(HEADS_PER_GROUP)], axis=1)
    z = jnp.zeros_like(qg)
    return jnp.concatenate([qg, z], axis=0) if g % 2 == 0 else jnp.concatenate([z, qg], axis=0)


def _qpos_lanes(qb, width):
    lane = lax.broadcasted_iota(I32, (1, width), 1)
    return qb * QB + (lane & (QB - 1))


def _store_heads(o_ref, g, accT, scale_row, gT_ref, branch):
    for r in range(HEADS_PER_GROUP):
        h = HEADS_PER_GROUP * g + r
        gate = gT_ref[h * N_BRANCH + branch:h * N_BRANCH + branch + 1, :]
        cols = slice(r * QB, (r + 1) * QB)
        o_ref[h * HEAD_DIM:(h + 1) * HEAD_DIM, :] = accT[:, cols] * (scale_row[:, cols] * gate)


def _cmp_select_body(qT_ref, gT_ref, kc_ref, vcT_ref, ovT_ref, o_ref, selT_ref, *, n_select):
    qb = pl.program_id(0)
    n_cmp = kc_ref.shape[1]
    n_sel = ovT_ref.shape[0]
    qpos = _qpos_lanes(qb, GQ)
    m_idx = lax.broadcasted_iota(I32, (n_cmp, 1), 0)
    usable = (m_idx >= 1) & (m_idx * CMP_STRIDE + (CMP_BLOCK - CMP_STRIDE - 1) <= qpos)
    blk = lax.broadcasted_iota(I32, (n_sel, QB), 0)
    qp = qpos[:, :QB]
    causal = blk * SEL_BLOCK <= qp
    dist = qp // SEL_BLOCK - blk
    forced = (blk == 0) | ((dist >= 0) & (dist < N_LOCAL_FORCED))
    for g in range(N_GROUPS):
        pair = g // 2
        sT = jnp.dot(kc_ref[0, :, pair * PAIR_W:(pair + 1) * PAIR_W], _q_aug(qT_ref, g),
                     preferred_element_type=F32)
        sT = jnp.where(usable, sT, -jnp.inf)
        mx = jnp.max(sT, axis=0, keepdims=True)
        mx = jnp.where(mx == -jnp.inf, 0.0, mx)
        e = jnp.exp(sT - mx)
        inv = 1.0 / jnp.maximum(jnp.sum(e, axis=0, keepdims=True), 1e-30)
        pb = (e * inv).astype(BF16)
        oT = jnp.dot(vcT_ref[0, g * HEAD_DIM:(g + 1) * HEAD_DIM, :], pb, preferred_element_type=F32)
        _store_heads(o_ref, g, oT, jnp.ones((1, GQ), F32), gT_ref, 0)
        imp = jnp.zeros((n_sel, QB), F32)
        for r in range(HEADS_PER_GROUP):
            imp = imp + jnp.dot(ovT_ref[...], pb[:, r * QB:(r + 1) * QB], preferred_element_type=F32)
        adj = jnp.where(causal, jnp.where(forced, POS_BIG, imp), NEG_BIG)

        def pick(_, carry):
            adj, bias = carry
            mx = jnp.max(adj, axis=0, keepdims=True)
            first = jnp.min(jnp.where(adj == mx, blk, n_sel), axis=0, keepdims=True)
            hit = blk == first
            return jnp.where(hit, -jnp.inf, adj), jnp.where(hit, 0.0, bias)

        _, bias = lax.fori_loop(0, n_select, pick, (adj, jnp.full((n_sel, QB), MASKED, F32)))
        selT_ref[g] = bias


def _overlap_T(n_cmp_rows, n_sel):
    m = np.arange(n_cmp_rows)[None, :] - 1
    s = np.arange(n_sel)[:, None]
    start_c = m * CMP_STRIDE
    ov = (m >= 0) & (start_c < s * SEL_BLOCK + SEL_BLOCK) & (start_c + CMP_BLOCK > s * SEL_BLOCK)
    return jnp.asarray(ov, BF16)


def prompt_cmp_select(qT, gT, kc, vcT, n_q):
    n_cmp = kc.shape[1]
    n_sel = n_q // SEL_BLOCK
    ovT = _overlap_T(n_cmp, n_sel)
    col = lambda i: (0, i)
    return pl.pallas_call(
        functools.partial(_cmp_select_body, n_select=min(N_SELECT, n_sel)),
        out_shape=(jax.ShapeDtypeStruct((NSA_WIDTH, n_q), F32), jax.ShapeDtypeStruct((N_GROUPS, n_sel, n_q), F32)),
        grid=(n_q // QB,),
        in_specs=[pl.BlockSpec((NSA_WIDTH, QB), col), pl.BlockSpec((LANES, QB), col),
                  pl.BlockSpec((1, n_cmp, KV_ROW), lambda i: (0, 0, 0)),
                  pl.BlockSpec((1, KV_WIDTH, n_cmp), lambda i: (0, 0, 0)),
                  pl.BlockSpec((n_sel, n_cmp), lambda i: (0, 0))],
        out_specs=(pl.BlockSpec((NSA_WIDTH, QB), col), pl.BlockSpec((N_GROUPS, n_sel, QB), lambda i: (0, 0, i))),
        compiler_params=_cparams(("parallel",)),
        name="prompt_cmp_select",
    )(qT, gT, kc, vcT, ovT)


KT = 512
MASKED = -1e30


def _online_softmax_step(s, m, l, acc, vT_tile):
    m_new = jnp.maximum(m, jnp.max(s, axis=0, keepdims=True))
    alpha = jnp.exp(m - m_new)
    e = jnp.exp(s - m_new)
    l = l * alpha + jnp.sum(e, axis=0, keepdims=True)
    acc = acc * alpha + jnp.dot(vT_tile, e.astype(BF16), preferred_element_type=F32)
    return m_new, l, acc


def _softmax_init():
    return (jnp.full((1, GQ), MASKED, F32), jnp.zeros((1, GQ), F32), jnp.zeros((HEAD_DIM, GQ), F32))


def _prompt_sel_body(qT_ref, gT_ref, selT_ref, k_ref, vT_ref, o_ref):
    qb = pl.program_id(0)
    last = (qb * QB + QB - 1) // KT
    qpos = _qpos_lanes(qb, GQ)
    krow = lax.broadcasted_iota(I32, (KT, 1), 0)
    blocks_per_tile = KT // SEL_BLOCK
    for g in range(N_GROUPS):
        pair = g // 2
        qa = _q_aug(qT_ref, g)

        def tile(kt, carry, diagonal, g=g, pair=pair, qa=qa):
            base = pl.multiple_of(kt * KT, KT)
            sT = jnp.dot(k_ref[pl.ds(base, KT), pair * PAIR_W:(pair + 1) * PAIR_W], qa, preferred_element_type=F32)
            rows = selT_ref[g, pl.ds(pl.multiple_of(kt * blocks_per_tile, blocks_per_tile), blocks_per_tile), :]
            bias = jnp.concatenate([rows] * HEADS_PER_GROUP, axis=1)
            s = (sT.reshape(blocks_per_tile, SEL_BLOCK, GQ) + bias[:, None, :]).reshape(KT, GQ)
            if diagonal:
                s = jnp.where(base + krow <= qpos, s, MASKED)
            vt = vT_ref[g * HEAD_DIM:(g + 1) * HEAD_DIM, pl.ds(base, KT)]
            return _online_softmax_step(s, *carry, vt)

        carry = lax.fori_loop(0, last, functools.partial(tile, diagonal=False), _softmax_init())
        m, l, acc = tile(last, carry, True)
        _store_heads(o_ref, g, acc, 1.0 / jnp.maximum(l, 1e-30), gT_ref, 1)


def prompt_sel_attention(qT, gT, selT, ks, vsT, n_q):
    n_sel = selT.shape[1]
    col = lambda i: (0, i)
    return pl.pallas_call(
        _prompt_sel_body,
        out_shape=jax.ShapeDtypeStruct((NSA_WIDTH, n_q), F32),
        grid=(n_q // QB,),
        in_specs=[pl.BlockSpec((NSA_WIDTH, QB), col), pl.BlockSpec((LANES, QB), col),
                  pl.BlockSpec((N_GROUPS, n_sel, QB), lambda i: (0, 0, i)),
                  pl.BlockSpec((n_q, KV_WIDTH), lambda i: (0, 0)),
                  pl.BlockSpec((KV_WIDTH, n_q), lambda i: (0, 0))],
        out_specs=pl.BlockSpec((NSA_WIDTH, QB), col),
        compiler_params=_cparams(("parallel",)),
        name="prompt_sel_attention",
    )(qT, gT, selT, ks, vsT)


WIN_SLAB = WINDOW + QB


def _prompt_win_body(qT_ref, gT_ref, k_ref, vT_ref, o_ref):
    qb = pl.program_id(0)
    qpos = _qpos_lanes(qb, GQ)
    base = pl.multiple_of(jnp.maximum(qb * QB - WINDOW, 0), QB)
    kpos = base + lax.broadcasted_iota(I32, (WIN_SLAB, 1), 0)
    valid = (kpos <= qpos) & (kpos > qpos - WINDOW)
    for g in range(N_GROUPS):
        pair = g // 2
        sT = jnp.dot(k_ref[pl.ds(base, WIN_SLAB), pair * PAIR_W:(pair + 1) * PAIR_W], _q_aug(qT_ref, g),
                     preferred_element_type=F32)
        s = jnp.where(valid, sT, MASKED)
        e = jnp.exp(s - jnp.max(s, axis=0, keepdims=True))
        l = jnp.sum(e, axis=0, keepdims=True)
        acc = jnp.dot(vT_ref[g * HEAD_DIM:(g + 1) * HEAD_DIM, pl.ds(base, WIN_SLAB)], e.astype(BF16),
                      preferred_element_type=F32)
        _store_heads(o_ref, g, acc, 1.0 / l, gT_ref, 2)


def prompt_win_attention(qT, gT, kw, vwT, n_q):
    col = lambda i: (0, i)
    return pl.pallas_call(
        _prompt_win_body,
        out_shape=jax.ShapeDtypeStruct((NSA_WIDTH, n_q), F32),
        grid=(n_q // QB,),
        in_specs=[pl.BlockSpec((NSA_WIDTH, QB), col), pl.BlockSpec((LANES, QB), col),
                  pl.BlockSpec((n_q, KV_WIDTH), lambda i: (0, 0)),
                  pl.BlockSpec((KV_WIDTH, n_q), lambda i: (0, 0))],
        out_specs=pl.BlockSpec((NSA_WIDTH, QB), col),
        compiler_params=_cparams(("parallel",)),
        name="prompt_win_attention",
    )(qT, gT, kw, vwT)


def _nt_dot(a, b):
    return lax.dot_general(a, b, (((1,), (1,)), ((), ())), preferred_element_type=F32)


def _rows_softmax_step(s, valid, m, l, acc, v, v_transposed=False):
    s = jnp.where(valid, s, MASKED)
    m_new = jnp.maximum(m, jnp.max(s, axis=1, keepdims=True))
    alpha = jnp.exp(m - m_new)
    e = jnp.where(valid, jnp.exp(s - m_new), 0.0)
    l = l * alpha + jnp.sum(e, axis=1, keepdims=True)
    eb = e.astype(BF16)
    pv = _nt_dot(eb, v) if v_transposed else jnp.dot(eb, v, preferred_element_type=F32)
    return m_new, l, acc * alpha + pv


def _sample_attn_body(pt_ref, *refs, past_len, n_tok, n_select):
    del pt_ref
    pages = refs[:PAGES_PER_STEP]
    (qa_ref, gate_ref, kc_ref, ov_ref, ksn_ref, win_ref, kwn_ref, o_ref,
     pick_ref, ocmp_ref, m_ref, l_ref, acc_ref) = refs[PAGES_PER_STEP:]
    st = pl.program_id(1)
    n_rows = N_HEADS * n_tok
    n_sel = ov_ref.shape[1]
    keys_per_step = PAGES_PER_STEP * PAGE
    qa = qa_ref[0]
    tok = lax.broadcasted_iota(I32, (n_rows, 1), 0) % n_tok
    qpos = past_len + tok

    @pl.when(st == 0)
    def _():
        n_cmp = kc_ref.shape[1]
        kc = kc_ref[0]
        m_idx = lax.broadcasted_iota(I32, (1, n_cmp), 1)
        usable = (m_idx >= 1) & (m_idx * CMP_STRIDE + (CMP_BLOCK - CMP_STRIDE - 1) <= qpos)
        s = jnp.where(usable, _nt_dot(qa, kc[:, :KV_WIDTH]), -jnp.inf)
        mx = jnp.max(s, axis=1, keepdims=True)
        mx = jnp.where(mx == -jnp.inf, 0.0, mx)
        e = jnp.exp(s - mx)
        pb = (e * (1.0 / jnp.maximum(jnp.sum(e, axis=1, keepdims=True), 1e-30))).astype(BF16)
        ocmp_ref[...] = jnp.dot(pb, kc[:, KV_WIDTH:], preferred_element_type=F32) * gate_ref[0, :, 0:1]
        imp_rows = jnp.dot(pb, ov_ref[...], preferred_element_type=F32)
        imp = imp_rows.reshape(N_GROUPS, HEADS_PER_GROUP, n_tok, n_sel).sum(axis=1).reshape(N_GROUPS * n_tok, n_sel)
        blk = lax.broadcasted_iota(I32, (N_GROUPS * n_tok, n_sel), 1)
        qp = past_len + lax.broadcasted_iota(I32, (N_GROUPS * n_tok, 1), 0) % n_tok
        dist = qp // SEL_BLOCK - blk
        forced = (blk == 0) | ((dist >= 0) & (dist < N_LOCAL_FORCED))
        adj = jnp.where(forced, POS_BIG, imp)

        def pick(_, carry):
            adj, sel = carry
            mx = jnp.max(adj, axis=1, keepdims=True)
            first = jnp.min(jnp.where(adj == mx, blk, n_sel), axis=1, keepdims=True)
            hit = blk == first
            return jnp.where(hit, -jnp.inf, adj), jnp.where(hit, 1.0, sel)

        _, sel = lax.fori_loop(0, n_select, pick, (adj, jnp.zeros((N_GROUPS * n_tok, n_sel), F32)))
        sel = jnp.broadcast_to(sel.reshape(N_GROUPS, 1, n_tok, n_sel), (N_GROUPS, HEADS_PER_GROUP, n_tok, n_sel))
        pick_ref[...] = sel.reshape(n_rows, n_sel).astype(BF16)
        m_ref[...] = jnp.full((n_rows, 1), MASKED, F32)
        l_ref[...] = jnp.zeros((n_rows, 1), F32)
        acc_ref[...] = jnp.zeros((n_rows, KV_WIDTH), F32)

    kvT = jnp.concatenate([p[0].astype(BF16) for p in pages], axis=1)
    s = jnp.dot(qa, kvT[:KV_WIDTH, :], preferred_element_type=F32)
    sel_row = lax.broadcasted_iota(I32, (n_sel, keys_per_step), 0)
    key_blk = st * (keys_per_step // SEL_BLOCK) + lax.broadcasted_iota(I32, (n_sel, keys_per_step), 1) // SEL_BLOCK
    expand = jnp.where(sel_row == key_blk, 1.0, 0.0).astype(BF16)
    valid = jnp.dot(pick_ref[...], expand, preferred_element_type=F32) > 0.5
    m, l, acc = _rows_softmax_step(s, valid, m_ref[...], l_ref[...], acc_ref[...], kvT[KV_WIDTH:, :], True)
    m_ref[...] = m
    l_ref[...] = l
    acc_ref[...] = acc

    @pl.when(st == pl.num_programs(1) - 1)
    def _():
        newer = lax.broadcasted_iota(I32, (1, n_tok), 1) <= tok
        ksn = ksn_ref[0].astype(BF16)
        m2, l2, acc2 = _rows_softmax_step(_nt_dot(qa, ksn[:, :KV_WIDTH]), newer, m, l, acc, ksn[:, KV_WIDTH:])
        o = ocmp_ref[...] + acc2 * (gate_ref[0, :, 1:2] / jnp.maximum(l2, 1e-30))
        n_buf = win_ref.shape[2]
        winT = win_ref[0].astype(BF16)
        wpos = past_len - n_buf + lax.broadcasted_iota(I32, (1, n_buf), 1)
        wvalid = (wpos <= qpos) & (wpos > qpos - WINDOW) & (wpos >= 0)
        init = (jnp.full((n_rows, 1), MASKED, F32), jnp.zeros((n_rows, 1), F32), jnp.zeros((n_rows, KV_WIDTH), F32))
        st_w = _rows_softmax_step(jnp.dot(qa, winT[:KV_WIDTH, :], preferred_element_type=F32), wvalid, *init,
                                  winT[KV_WIDTH:, :], True)
        kwn = kwn_ref[0].astype(BF16)
        _, l3, acc3 = _rows_softmax_step(_nt_dot(qa, kwn[:, :KV_WIDTH]), newer, *st_w, kwn[:, KV_WIDTH:])
        o = o + acc3 * (gate_ref[0, :, 2:3] / jnp.maximum(l3, 1e-30))
        for h in range(N_HEADS):
            g = h // HEADS_PER_GROUP
            o_ref[0, :, h * HEAD_DIM:(h + 1) * HEAD_DIM] = o[h * n_tok:(h + 1) * n_tok, g * HEAD_DIM:(g + 1) * HEAD_DIM]


def _sample_q_layout(qT_s, gT_s, B, n_tok):
    q = qT_s.T.reshape(B, n_tok, N_HEADS, HEAD_DIM).transpose(0, 2, 1, 3)
    own = (jnp.arange(N_HEADS)[:, None] // HEADS_PER_GROUP == jnp.arange(N_GROUPS)[None, :]).astype(q.dtype)
    qa = (q[:, :, :, None, :] * own[None, :, None, :, None]).reshape(B, N_HEADS * n_tok, KV_WIDTH)
    g = gT_s[:N_HEADS * N_BRANCH].reshape(N_HEADS, N_BRANCH, B, n_tok).transpose(2, 0, 3, 1)
    g = jnp.pad(g.reshape(B, N_HEADS * n_tok, N_BRANCH), ((0, 0), (0, 0), (0, LANES - N_BRANCH)))
    return qa, g


def sample_attention(qa, gates, kc, cache_sel, table, ks_new, win_buf, kw_new, past_len):
    B, n_pages = table.shape
    n_tok = ks_new.shape[1]
    n_rows = N_HEADS * n_tok
    n_cmp = kc.shape[1]
    n_buf = win_buf.shape[2]
    assert n_pages % PAGES_PER_STEP == 0 and past_len == n_pages * PAGE and past_len % SEL_BLOCK == 0 and n_tok <= SEL_BLOCK
    n_sel = past_len // SEL_BLOCK
    n_steps = n_pages // PAGES_PER_STEP
    n_select = min(N_SELECT, n_sel + 1) - 1
    ov = _overlap_T(n_cmp, n_sel).T
    flat = table.reshape(-1).astype(I32)

    def page_map(k):
        return lambda b, st, pt: (pt[b * n_pages + st * PAGES_PER_STEP + k], 0, 0)

    per_b = lambda b, st, pt: (b, 0, 0)
    in_specs = [pl.BlockSpec((1, KV_ROW, PAGE), page_map(k)) for k in range(PAGES_PER_STEP)]
    in_specs += [pl.BlockSpec((1, n_rows, KV_WIDTH), per_b), pl.BlockSpec((1, n_rows, LANES), per_b),
                 pl.BlockSpec((1, n_cmp, KV_ROW), per_b), pl.BlockSpec((n_cmp, n_sel), lambda b, st, pt: (0, 0)),
                 pl.BlockSpec((1, n_tok, KV_ROW), per_b), pl.BlockSpec((1, KV_ROW, n_buf), per_b),
                 pl.BlockSpec((1, n_tok, KV_ROW), per_b)]
    grid_spec = pltpu.PrefetchScalarGridSpec(
        num_scalar_prefetch=1,
        grid=(B, n_steps),
        in_specs=in_specs,
        out_specs=pl.BlockSpec((1, n_tok, NSA_WIDTH), per_b),
        scratch_shapes=[pltpu.VMEM((n_rows, n_sel), BF16), pltpu.VMEM((n_rows, KV_WIDTH), F32),
                        pltpu.VMEM((n_rows, 1), F32), pltpu.VMEM((n_rows, 1), F32),
                        pltpu.VMEM((n_rows, KV_WIDTH), F32)],
    )
    return pl.pallas_call(
        functools.partial(_sample_attn_body, past_len=past_len, n_tok=n_tok, n_select=n_select),
        out_shape=jax.ShapeDtypeStruct((B, n_tok, NSA_WIDTH), F32),
        grid_spec=grid_spec,
        compiler_params=_cparams(("parallel", "arbitrary")),
        name="sample_attention",
    )(flat, *([cache_sel] * PAGES_PER_STEP), qa, gates, kc, ov, ks_new, win_buf, kw_new)


HALO = 16


def _pool_body(halo_ref, cur_ref, w_ref, b_ref, scale_ref, o_ref, ctx_ref, *, n_prev, seq_tiles, halo_is_self):
    i = pl.program_id(0)
    tm = cur_ref.shape[0]
    cur = cur_ref[...]
    halo = halo_ref[...]
    if halo_is_self:
        halo = jnp.where(i % seq_tiles == 0, 0.0, halo)
    ctx_ref[0:HALO, :] = halo
    ctx_ref[HALO:HALO + tm, :] = cur
    t_abs = (i % seq_tiles) * tm + lax.broadcasted_iota(I32, (tm, 1), 0)
    outs = []
    for g, w in enumerate(POOL_WINDOWS):
        lo, hi = g * POOL_GROUP_IN, (g + 1) * POOL_GROUP_IN
        acc = cur[:, lo:hi]
        for k in range(1, w):
            acc = acc + ctx_ref[HALO - k:HALO - k + tm, lo:hi]
        cnt = jnp.minimum(w, t_abs + 1 + n_prev).astype(F32)
        pooled = (acc / cnt - cur[:, lo:hi]).astype(BF16)
        outs.append(jnp.dot(pooled, w_ref[g], preferred_element_type=F32) + b_ref[g:g + 1, :])
    o_ref[...] = jnp.concatenate(outs, axis=1) * scale_ref[...]


def pool_mixer(u, halo_src, pool_w, pool_b, pool_scale, *, tm, row0, n_rows, n_prev, halo_is_self):
    n_tiles = n_rows // tm
    assert n_rows % tm == 0 and row0 % tm == 0 and tm % SUBLANES == 0
    if halo_is_self:
        assert tm % HALO == 0
        halo_map = lambda i: (jnp.maximum((row0 + i * tm) // HALO - 1, 0), 0)
        seq_tiles = n_tiles
    else:
        halo_map = lambda i: (i, 0)
        seq_tiles = 1
    return pl.pallas_call(
        functools.partial(_pool_body, n_prev=n_prev, seq_tiles=seq_tiles, halo_is_self=halo_is_self),
        out_shape=jax.ShapeDtypeStruct((n_rows, D_MODEL), F32),
        grid=(n_tiles,),
        in_specs=[pl.BlockSpec((HALO, POOL_WIDTH), halo_map),
                  pl.BlockSpec((tm, POOL_WIDTH), lambda i: (row0 // tm + i, 0)),
                  pl.BlockSpec((POOL_GROUPS, POOL_GROUP_IN, POOL_GROUP_OUT), lambda i: (0, 0, 0)),
                  pl.BlockSpec((POOL_GROUPS, POOL_GROUP_OUT), lambda i: (0, 0)),
                  pl.BlockSpec((1, D_MODEL), lambda i: (0, 0))],
        out_specs=pl.BlockSpec((tm, D_MODEL), lambda i: (i, 0)),
        scratch_shapes=[pltpu.VMEM((HALO + tm, POOL_WIDTH), F32)],
        compiler_params=_cparams(("arbitrary",)),
        name="pool_mixer",
    )(halo_src, u, pool_w, pool_b, pool_scale)


def _layer_norm(v, g, b):
    mu = jnp.mean(v, axis=-1, keepdims=True)
    c = v - mu
    var = jnp.mean(c * c, axis=-1, keepdims=True)
    return c * lax.rsqrt(var + LN_EPS) * g + b


def _merge_body(oc_ref, os_ref, ow_ref, osamp_ref, pbp_ref, pbs_ref, bga_ref, bgb_ref, x_ref, wa_ref, wo_ref,
                g_ref, b_ref, rw_ref, rb_ref, h_ref, te_ref, tg_ref, o_sc, pb_sc, *, n_prompt_tiles, alpha):
    i = pl.program_id(0)

    @pl.when(i < n_prompt_tiles)
    def _():
        o_sc[...] = (oc_ref[...] + os_ref[...] + ow_ref[...]).T.astype(BF16)
        pb_sc[...] = pbp_ref[...]

    @pl.when(i >= n_prompt_tiles)
    def _():
        o_sc[...] = osamp_ref[...].astype(BF16)
        pb_sc[...] = pbs_ref[...]

    a = jnp.dot(o_sc[...], wa_ref[...], preferred_element_type=F32)
    mixed = (bga_ref[...] * a + bgb_ref[...] * pb_sc[...]).astype(BF16)
    mix = jnp.dot(mixed, wo_ref[...], preferred_element_type=F32)
    h = _layer_norm(alpha * x_ref[...] + mix, g_ref[...], b_ref[...])
    h_ref[...] = h
    logits = jnp.dot(h.astype(BF16), rw_ref[...], preferred_element_type=F32) + rb_ref[...]
    lane = lax.broadcasted_iota(I32, logits.shape, 1)
    picks_e = jnp.zeros(logits.shape, I32)
    picks_l = jnp.full(logits.shape, -jnp.inf, F32)
    for k in range(TOP_K):
        mx = jnp.max(logits, axis=1, keepdims=True)
        idx = jnp.min(jnp.where(logits == mx, lane, LANES), axis=1, keepdims=True)
        picks_e = jnp.where(lane == k, idx, picks_e)
        picks_l = jnp.where(lane == k, mx, picks_l)
        logits = jnp.where(lane == idx, -jnp.inf, logits)
    ex = jnp.exp(picks_l - jnp.max(picks_l, axis=1, keepdims=True))
    te_ref[...] = picks_e
    tg_ref[...] = ex / jnp.sum(ex, axis=1, keepdims=True)


def merge_norm_route(ocT, osT, owT, o_samp, pb_p, pb_s, bg, x_all, wa, wo, ln_g, ln_b, rw, rb, *, tm, n_prompt, alpha):
    T = x_all.shape[0]
    n_pt = n_prompt // tm
    n_tiles = T // tm
    assert n_prompt % tm == 0 and T % tm == 0 and o_samp.shape[0] == T - n_prompt
    pcol = lambda i: (0, jnp.minimum(i, n_pt - 1))
    prow = lambda i: (jnp.minimum(i, n_pt - 1), 0)
    srow = lambda i: (jnp.maximum(i - n_pt, 0), 0)
    row = lambda i: (i, 0)
    const = lambda i: (0, 0)
    return pl.pallas_call(
        functools.partial(_merge_body, n_prompt_tiles=n_pt, alpha=alpha),
        out_shape=(jax.ShapeDtypeStruct((T, D_MODEL), F32), jax.ShapeDtypeStruct((T, LANES), I32),
                   jax.ShapeDtypeStruct((T, LANES), F32)),
        grid=(n_tiles,),
        in_specs=[pl.BlockSpec((NSA_WIDTH, tm), pcol), pl.BlockSpec((NSA_WIDTH, tm), pcol),
                  pl.BlockSpec((NSA_WIDTH, tm), pcol), pl.BlockSpec((tm, NSA_WIDTH), srow),
                  pl.BlockSpec((tm, D_MODEL), prow), pl.BlockSpec((tm, D_MODEL), srow),
                  pl.BlockSpec((tm, D_MODEL), lambda i: (i, 0)), pl.BlockSpec((tm, D_MODEL), lambda i: (i, 1)),
                  pl.BlockSpec((tm, D_MODEL), row),
                  pl.BlockSpec((NSA_WIDTH, D_MODEL), const), pl.BlockSpec((D_MODEL, D_MODEL), const),
                  pl.BlockSpec((1, D_MODEL), const), pl.BlockSpec((1, D_MODEL), const),
                  pl.BlockSpec((D_MODEL, LANES), const), pl.BlockSpec((1, LANES), const)],
        out_specs=(pl.BlockSpec((tm, D_MODEL), row), pl.BlockSpec((tm, LANES), row), pl.BlockSpec((tm, LANES), row)),
        scratch_shapes=[pltpu.VMEM((tm, NSA_WIDTH), BF16), pltpu.VMEM((tm, D_MODEL), F32)],
        compiler_params=_cparams(("parallel",)),
        name="merge_norm_route",
    )(ocT, osT, owT, o_samp, pb_p, pb_s, bg, bg, x_all, wa, wo, ln_g, ln_b, rw, rb)


MOE_ROWS = 1024
FF_TILE = 256


def _moe_body(be_ref, na_ref, tok_ref, tok_next_ref, h_hbm, gate_ref, wg_ref, wu_ref, bg_ref, bu_ref, wd_ref, bd_ref,
              o_ref, xg_ref, xb_ref, act_ref, sem, *, n_f):
    del be_ref
    blk = pl.program_id(0)
    j = pl.program_id(1)
    active = blk < na_ref[0]
    slot = blk % 2
    rows_per_step = MOE_ROWS // (2 * n_f)

    def row_copy(buf, r, src_row):
        return pltpu.make_async_copy(h_hbm.at[pl.ds(src_row, 1), :], xg_ref.at[buf, pl.ds(r, 1), :], sem.at[buf])

    @pl.when(active & (j == 0))
    def _():
        @pl.when(blk == 0)
        def _():
            def issue(r, c):
                row_copy(0, r, tok_ref[0, 0, r]).start()
                return c

            lax.fori_loop(0, MOE_ROWS, issue, 0)

        def drain(r, c):
            row_copy(slot, r, 0).wait()
            return c

        lax.fori_loop(0, MOE_ROWS, drain, 0, unroll=8)
        xb_ref[...] = xg_ref[slot].astype(BF16)

    @pl.when(blk + 1 < na_ref[0])
    def _():
        for r in range(rows_per_step):
            row = j * rows_per_step + r
            row_copy(1 - slot, row, tok_next_ref[0, 0, row]).start()

    @pl.when(active & (j < n_f))
    def _():
        xb = xb_ref[...]
        hg = jnp.dot(xb, wg_ref[0].astype(BF16), preferred_element_type=F32) + bg_ref[0]
        hu = jnp.dot(xb, wu_ref[0].astype(BF16), preferred_element_type=F32) + bu_ref[0]
        gt = jnp.minimum(hg, SWIGLU_LIMIT)
        up = jnp.clip(hu, -SWIGLU_LIMIT, SWIGLU_LIMIT)
        act = (up + 1.0) * gt * jax.nn.sigmoid(SWIGLU_ALPHA * gt)
        act_ref[:, pl.ds(pl.multiple_of(j * FF_TILE, FF_TILE), FF_TILE)] = act.astype(BF16)

    @pl.when(active & (j >= n_f))
    def _():
        y = jnp.dot(act_ref[...], wd_ref[0].astype(BF16), preferred_element_type=F32)
        o_ref[...] = (y + bd_ref[0]) * gate_ref[...]

    @pl.when(jnp.logical_not(active) & (j >= n_f))
    def _():
        o_ref[...] = jnp.zeros_like(o_ref)


def moe_experts(h, row_tok, row_gate, blk_expert, n_active, w_up, b_up, w_down, b_down):
    n_blocks = row_tok.shape[0]
    E, D, F2 = w_up.shape
    d_ff = F2 // 2
    n_f = d_ff // FF_TILE
    n_rows = n_blocks * MOE_ROWS

    assert MOE_ROWS % (2 * n_f) == 0 and w_down.shape == (E, d_ff, D) and D % FF_TILE == 0 and D // FF_TILE == n_f

    def e_of(b, be, na):
        return be[jnp.minimum(b, na[0] - 1)]

    def up_tile(b, j, na):
        return jnp.where(b < na[0], jnp.minimum(j, n_f - 1), n_f - 1)

    def down_tile(b, j, na):
        return jnp.where(b < na[0], jnp.maximum(j - n_f, 0), n_f - 1)

    b_up3 = b_up.reshape(E, 1, F2)
    b_down3 = b_down.reshape(E, 1, D)
    tok3 = row_tok.reshape(n_blocks, 1, MOE_ROWS)
    grid_spec = pltpu.PrefetchScalarGridSpec(
        num_scalar_prefetch=2,
        grid=(n_blocks, 2 * n_f),
        in_specs=[
            pl.BlockSpec((1, 1, MOE_ROWS), lambda b, j, be, na: (b, 0, 0), memory_space=pltpu.SMEM),
            pl.BlockSpec((1, 1, MOE_ROWS), lambda b, j, be, na: (jnp.minimum(b + 1, n_blocks - 1), 0, 0),
                         memory_space=pltpu.SMEM),
            pl.BlockSpec(memory_space=pl.ANY),
            pl.BlockSpec((MOE_ROWS, 1), lambda b, j, be, na: (b, 0)),
            pl.BlockSpec((1, D, FF_TILE), lambda b, j, be, na: (e_of(b, be, na), 0, up_tile(b, j, na))),
            pl.BlockSpec((1, D, FF_TILE), lambda b, j, be, na: (e_of(b, be, na), 0, n_f + up_tile(b, j, na))),
            pl.BlockSpec((1, 1, FF_TILE), lambda b, j, be, na: (e_of(b, be, na), 0, up_tile(b, j, na))),
            pl.BlockSpec((1, 1, FF_TILE), lambda b, j, be, na: (e_of(b, be, na), 0, n_f + up_tile(b, j, na))),
            pl.BlockSpec((1, d_ff, FF_TILE), lambda b, j, be, na: (e_of(b, be, na), 0, down_tile(b, j, na))),
            pl.BlockSpec((1, 1, FF_TILE), lambda b, j, be, na: (e_of(b, be, na), 0, down_tile(b, j, na))),
        ],
        out_specs=pl.BlockSpec((MOE_ROWS, FF_TILE), lambda b, j, be, na: (b, jnp.maximum(j - n_f, 0))),
        scratch_shapes=[pltpu.VMEM((2, MOE_ROWS, D), F32), pltpu.VMEM((MOE_ROWS, D), BF16),
                        pltpu.VMEM((MOE_ROWS, d_ff), BF16), pltpu.SemaphoreType.DMA((2,))],
    )
    return pl.pallas_call(
        functools.partial(_moe_body, n_f=n_f),
        out_shape=jax.ShapeDtypeStruct((n_rows, D), F32),
        grid_spec=grid_spec,
        compiler_params=_cparams(("arbitrary", "arbitrary")),
        name="moe_experts",
    )(blk_expert, n_active, tok3, tok3, h, row_gate, w_up, w_up, b_up3, b_up3, w_down, b_down3)


def _moe_dispatch(top_e, top_g, n_experts):
    T = top_e.shape[0]
    n_pairs = T * TOP_K
    e_flat = top_e.reshape(-1)
    order = jnp.argsort(e_flat)
    e_sorted = e_flat[order]
    counts = jnp.bincount(e_flat, length=n_experts)
    padded = (counts + MOE_ROWS - 1) // MOE_ROWS * MOE_ROWS
    pad_end = jnp.cumsum(padded)
    pad_start = pad_end - padded
    start = jnp.cumsum(counts) - counts
    dest = pad_start[e_sorted] + (jnp.arange(n_pairs) - start[e_sorted])
    n_blocks = -(-n_pairs // MOE_ROWS) + n_experts
    n_rows = n_blocks * MOE_ROWS
    tok_sorted = (order // TOP_K).astype(I32)
    row_tok = jnp.zeros((n_rows,), I32).at[dest].set(tok_sorted)
    row_gate = jnp.zeros((n_rows,), F32).at[dest].set(top_g.reshape(-1)[order])
    blk_expert = jnp.clip(jnp.searchsorted(pad_end, jnp.arange(n_blocks) * MOE_ROWS, side='right'), 0, n_experts - 1)
    n_active = (pad_end[-1] // MOE_ROWS).astype(I32).reshape(1)
    pair_row = jnp.zeros((n_pairs,), I32).at[order].set(dest.astype(I32))
    return (row_tok.reshape(n_blocks, MOE_ROWS), row_gate.reshape(n_rows, 1), blk_expert.astype(I32), n_active,
            pair_row.reshape(T, TOP_K))


def _combine_body(pos_ref, rows_hbm, h_ref, g_ref, b_ref, y_ref, buf_ref, sem, *, alpha):
    tm = h_ref.shape[0]

    def row_copy(t, k, src_row):
        return pltpu.make_async_copy(rows_hbm.at[pl.ds(src_row, 1), :], buf_ref.at[k, pl.ds(t, 1), :], sem)

    def issue(t, c):
        for k in range(TOP_K):
            row_copy(t, k, pos_ref[0, 0, t * TOP_K + k]).start()
        return c

    def drain(t, c):
        for k in range(TOP_K):
            row_copy(t, k, 0).wait()
        return c

    lax.fori_loop(0, tm, issue, 0)
    lax.fori_loop(0, tm, drain, 0)
    moe = buf_ref[0]
    for k in range(1, TOP_K):
        moe = moe + buf_ref[k]
    h = h_ref[...]
    y_ref[...] = _layer_norm(alpha * h + moe, g_ref[...], b_ref[...])


def combine_norm(out_rows, pair_row, h, ln_g, ln_b, *, tm, alpha):
    T, D = h.shape
    assert T % tm == 0
    pos = pair_row.reshape(T // tm, 1, tm * TOP_K)
    return pl.pallas_call(
        functools.partial(_combine_body, alpha=alpha),
        out_shape=jax.ShapeDtypeStruct((T, D), F32),
        grid=(T // tm,),
        in_specs=[pl.BlockSpec((1, 1, tm * TOP_K), lambda i: (i, 0, 0), memory_space=pltpu.SMEM),
                  pl.BlockSpec(memory_space=pl.ANY),
                  pl.BlockSpec((tm, D), lambda i: (i, 0)),
                  pl.BlockSpec((1, D), lambda i: (0, 0)), pl.BlockSpec((1, D), lambda i: (0, 0))],
        out_specs=pl.BlockSpec((tm, D), lambda i: (i, 0)),
        scratch_shapes=[pltpu.VMEM((TOP_K, tm, D), F32), pltpu.SemaphoreType.DMA(())],
        compiler_params=_cparams(("arbitrary",)),
        name="combine_norm",
    )(pos, out_rows, h, ln_g, ln_b)


def _largest_tile(n, candidates):
    for c in candidates:
        if n % c == 0:
            return c
    raise ValueError(f"no tile of {candidates} divides {n}")


def _layer(xp, xs, cache_cmp, cache_sel, win_buf, pool_buf, table, w_in, cmp_pos, cmp_w, w_branch_a, pool_w, pool_b,
           pool_scale, w_out, ln1_g, ln1_b, router_w, router_b, w_up, b_up, w_down, b_down, ln2_g, ln2_b, alpha):
    TP = xp.shape[0]
    B, n_tok, _ = xs.shape
    TS = B * n_tok
    T = TP + TS
    n_experts = router_w.shape[1]
    n_pages = table.shape[1]
    past_len = n_pages * PAGE
    assert T % PAGE == 0 and TP % (PAGE * PAGES_PER_STEP) == 0 and TP >= WINDOW and TP >= POOL_BUF

    x_all = jnp.concatenate([xp, xs.reshape(TS, D_MODEL)], axis=0)
    (qT, kvcT, kvsT, ks, vsT, kvwT, kw, vwT, gT, u, bg) = in_projection(
        x_all, _prep_w_in(w_in), _largest_tile(T, (640, 512, 384, 256, 128)))
    pos_rows, bdk, bdv = _prep_cmp(cmp_pos, cmp_w)

    kc_p, vcT_p = compress_kv(kvcT[None], jnp.arange(TP // PAGE, dtype=I32)[None], pos_rows, bdk, bdv,
                              pages_on_lanes=True)
    ocT, selT = prompt_cmp_select(qT, gT, kc_p, vcT_p, TP)
    osT = prompt_sel_attention(qT, gT, selT, ks, vsT, TP)
    owT = prompt_win_attention(qT, gT, kw, vwT, TP)

    kc_s, _ = compress_kv(cache_cmp, table, pos_rows, bdk, bdv, pages_on_lanes=False)
    qa, gates = _sample_q_layout(qT[:, TP:], gT[:, TP:], B, n_tok)
    kvc_new = kvcT[:, TP:].T.reshape(B, n_tok, KV_ROW)
    ks_new = kvsT[:, TP:].T.reshape(B, n_tok, KV_ROW)
    kw_new = kvwT[:, TP:].T.reshape(B, n_tok, KV_ROW)
    o_s = sample_attention(qa, gates, kc_s, cache_sel, table, ks_new, win_buf, kw_new, past_len).reshape(TS, NSA_WIDTH)

    pw = pool_w.astype(BF16)
    scale = pool_scale.reshape(1, D_MODEL)
    pb_p = pool_mixer(u, u, pw, pool_b, scale, tm=_largest_tile(TP, (512, 256, 128)), row0=0, n_rows=TP, n_prev=0,
                      halo_is_self=True)
    halo_s = jnp.concatenate([jnp.zeros((B, HALO - POOL_BUF, POOL_WIDTH), F32), pool_buf], axis=1)
    pb_s = pool_mixer(u, halo_s.reshape(B * HALO, POOL_WIDTH), pw, pool_b, scale, tm=n_tok, row0=TP, n_rows=TS,
                      n_prev=POOL_BUF, halo_is_self=False)

    rw = jnp.pad(router_w, ((0, 0), (0, LANES - n_experts))).astype(BF16)
    rb = jnp.pad(router_b, (0, LANES - n_experts), constant_values=-jnp.inf).reshape(1, LANES)
    tm_tok = _largest_tile(TS, (256, 128))
    h, top_e, top_g = merge_norm_route(
        ocT, osT, owT, o_s, pb_p, pb_s, bg, x_all, w_branch_a.astype(BF16), w_out.astype(BF16),
        ln1_g.reshape(1, D_MODEL), ln1_b.reshape(1, D_MODEL), rw, rb, tm=tm_tok, n_prompt=TP, alpha=alpha)
    row_tok, row_gate, blk_expert, n_active, pair_row = _moe_dispatch(top_e[:, :TOP_K], top_g[:, :TOP_K], n_experts)
    out_rows = moe_experts(h, row_tok, row_gate, blk_expert, n_active, w_up, b_up, w_down, b_down)
    y = combine_norm(out_rows, pair_row, h, ln2_g.reshape(1, D_MODEL), ln2_b.reshape(1, D_MODEL), tm=tm_tok, alpha=alpha)

    rows_of = lambda aT: aT.reshape(2, N_GROUPS, HEAD_DIM, aT.shape[-1]).transpose(3, 0, 1, 2)
    new5 = lambda a: a.reshape(B, n_tok, 2, N_GROUPS, HEAD_DIM)
    u_s = u[TP:].reshape(B, n_tok, POOL_WIDTH)
    kw_newT = kvwT[:, TP:].reshape(KV_ROW, B, n_tok).transpose(1, 0, 2)
    win_sT = jnp.concatenate([win_buf[:, :, n_tok:], kw_newT], axis=2)
    return dict(
        y_p=y[:TP], y_s=y[TP:].reshape(B, n_tok, D_MODEL),
        kvc_p=rows_of(kvcT[:, :TP]), kvc_s=new5(kvc_new),
        kvs_p=rows_of(kvsT[:, :TP]), kvs_s=new5(ks_new),
        win_p=rows_of(kvwT[:, TP - WINDOW:TP]),
        win_s=win_sT.reshape(B, 2, N_GROUPS, HEAD_DIM, -1).transpose(0, 4, 1, 2, 3),
        pool_p=u[TP - POOL_BUF:TP],
        pool_s=jnp.concatenate([pool_buf, u_s], axis=1)[:, -POOL_BUF:],
    )


def kernel(x_prompt, x_sample, cache_cmp_kv, cache_sel_kv, state_win_kv, state_pool, page_table, w_in, cmp_pos, cmp_w, w_branch_a, pool_w, pool_b, pool_scale, w_out, ln1_g, ln1_b, router_w, router_b, w_up, b_up, w_down, b_down, ln2_g, ln2_b):
    depth = w_in.shape[0]
    assert x_prompt.shape[0] == 1, "one prompt sequence"
    alpha = (2.0 * depth) ** 0.25
    n_phys = cache_cmp_kv.shape[1]
    B = x_sample.shape[0]
    xp, xs = x_prompt[0], x_sample
    per_layer = []
    assert cache_cmp_kv.shape[2] == PAGE
    rowsT = lambda a: a.transpose(0, 2, 3, 4, 1).reshape(a.shape[0], KV_ROW, a.shape[1])
    for l in range(depth):
        r = _layer(xp, xs, rowsT(cache_cmp_kv[l]), rowsT(cache_sel_kv[l]),
                   rowsT(state_win_kv[l]), state_pool[l], page_table, w_in[l], cmp_pos[l], cmp_w[l],
                   w_branch_a[l], pool_w[l], pool_b[l], pool_scale[l], w_out[l], ln1_g[l], ln1_b[l], router_w[l],
                   router_b[l], w_up[l], b_up[l], w_down[l], b_down[l], ln2_g[l], ln2_b[l], alpha)
        xp, xs = r["y_p"], r["y_s"]
        per_layer.append(r)
    stack = lambda k, lead: jnp.stack([r[k][None] if lead else r[k] for r in per_layer])
    return (xp[None], xs, stack("kvc_p", True), stack("kvc_s", False), stack("kvs_p", True), stack("kvs_s", False),
            stack("win_p", True), stack("win_s", False), stack("pool_p", True), stack("pool_s", False))
```
